```python
import math
import jax, jax.numpy as jnp
from jax import lax
import numpy as np

D_MODEL = 1024
BATCH = 16
SEQ = 2048
DEPTH = 1

D_MIX = D_MODEL
D_CONV = D_MIX // 2
CONV_GROUPS = 8
CONV_WIDTH = 3
D_DELTA = D_MIX - D_CONV
DN_HEADS = 4
DN_HEAD_DIM = D_DELTA // DN_HEADS
DN_CONV_WIDTH = 4
DN_CHUNK = 64
N_GROUPS = 4
EXPERTS_PER_GROUP = 8
N_EXPERTS = N_GROUPS * EXPERTS_PER_GROUP
TOP_K = 2
D_EXPERT = D_MODEL // 4
MOE_BLOCK = 128
LN_EPS = 1e-5
RMS_EPS = 1e-6
DEEP_ALPHA = (2 * DEPTH) ** 0.25
DEEP_BETA = (8 * DEPTH) ** -0.25

IN_SPLITS = [D_CONV, D_CONV, D_CONV, 3 * D_DELTA, D_DELTA, DN_HEADS, DN_HEADS]
D_IN = sum(IN_SPLITS)

kernel_name = "hymba_conv_gdn_hiermoe_deepnorm_adaln"


def layer_norm(x, g, b):
    xf = x.astype(jnp.float32)
    mu = xf.mean(-1, keepdims=True)
    var = jnp.square(xf - mu).mean(-1, keepdims=True)
    return ((xf - mu) * lax.rsqrt(var + LN_EPS) * g + b).astype(x.dtype)


def rms_normalize(x):
    xf = x.astype(jnp.float32)
    return xf * lax.rsqrt(jnp.square(xf).mean(-1, keepdims=True) + RMS_EPS)


def l2_normalize(x):
    return x * lax.rsqrt(jnp.square(x).sum(-1, keepdims=True) + RMS_EPS)


def causal_depthwise_conv(x, w):
    K, C = w.shape
    return lax.conv_general_dilated(
        x, w[:, None, :].astype(x.dtype), window_strides=(1,), padding=[(K - 1, 0)],
        dimension_numbers=("NWC", "WIO", "NWC"), feature_group_count=C)


def short_conv_mixer(b_gate, c_gate, u, conv_w, norm_w):
    y = b_gate * causal_depthwise_conv(c_gate * u, conv_w)
    B_, L, _ = y.shape
    y = rms_normalize(y.reshape(B_, L, CONV_GROUPS, D_CONV // CONV_GROUPS)).reshape(B_, L, D_CONV)
    return (y * norm_w).astype(u.dtype)


def gated_delta_rule_chunked(q, k, v, g, beta):
    B_, H, L, dk = q.shape
    dv = v.shape[-1]
    C = DN_CHUNK
    N = L // C
    q = q.reshape(B_, H, N, C, dk)
    k = k.reshape(B_, H, N, C, dk)
    v = v.reshape(B_, H, N, C, dv)
    beta = beta.reshape(B_, H, N, C)
    g = jnp.cumsum(g.reshape(B_, H, N, C), axis=-1)
    idx = jnp.arange(C)
    lower_strict = idx[:, None] > idx[None, :]
    lower_incl = idx[:, None] >= idx[None, :]
    gamma = jnp.exp(jnp.where(lower_incl, g[..., :, None] - g[..., None, :], -jnp.inf))
    kk = jnp.einsum("bhncd,bhnsd->bhncs", k, k)
    l_mat = jnp.where(lower_strict, beta[..., :, None] * kk * gamma, 0.0)
    eye = jnp.eye(C, dtype=jnp.float32)
    rhs = jnp.concatenate([beta[..., None] * v, (beta * jnp.exp(g))[..., None] * k], axis=-1)
    sol = lax.linalg.triangular_solve(eye + l_mat, rhs, left_side=True, lower=True, unit_diagonal=True)
    u, w = sol[..., :dv], sol[..., dv:]
    qk = jnp.where(lower_incl, jnp.einsum("bhncd,bhnsd->bhncs", q, k) * gamma, 0.0)
    q_decay = q * jnp.exp(g)[..., None]
    k_to_end = k * jnp.exp(g[..., -1:] - g)[..., None]
    chunk_decay = jnp.exp(g[..., -1])

    def step(S, xs):
        u_c, w_c, qk_c, qd_c, ke_c, dec_c = xs
        delta = u_c - jnp.einsum("bhcd,bhde->bhce", w_c, S)
        o = jnp.einsum("bhcd,bhde->bhce", qd_c, S) + jnp.einsum("bhcs,bhse->bhce", qk_c, delta)
        S = S * dec_c[..., None, None] + jnp.einsum("bhcd,bhce->bhde", ke_c, delta)
        return S, o

    xs = tuple(jnp.moveaxis(a, 2, 0) for a in (u, w, qk, q_decay, k_to_end, chunk_decay))
    S0 = jnp.zeros((B_, H, dk, dv), jnp.float32)
    _, o = lax.scan(step, S0, xs)
    return jnp.moveaxis(o, 0, 2).reshape(B_, H, L, dv)


def gated_deltanet(qkv, z, beta_logit, a_logit, conv_w, A_log, dt_bias, norm_w):
    B_, L, _ = qkv.shape
    qkv = jax.nn.silu(causal_depthwise_conv(qkv, conv_w))
    q, k, v = jnp.split(qkv, 3, axis=-1)

    def heads(t):
        return t.reshape(B_, L, DN_HEADS, DN_HEAD_DIM).transpose(0, 2, 1, 3).astype(jnp.float32)

    qh = l2_normalize(heads(q)) * (DN_HEAD_DIM ** -0.5)
    kh = l2_normalize(heads(k))
    vh = heads(v)
    beta = jax.nn.sigmoid(beta_logit.astype(jnp.float32)).transpose(0, 2, 1)
    g = (-jnp.exp(A_log.astype(jnp.float32))
         * jax.nn.softplus(a_logit.astype(jnp.float32) + dt_bias.astype(jnp.float32))).transpose(0, 2, 1)
    o = gated_delta_rule_chunked(qh, kh, vh, g, beta).transpose(0, 2, 1, 3)
    zf = z.reshape(B_, L, DN_HEADS, DN_HEAD_DIM).astype(jnp.float32)
    o = rms_normalize(o) * norm_w * jax.nn.silu(zf)
    return o.reshape(B_, L, D_DELTA).astype(qkv.dtype)


def hierarchical_moe(h, w_grp, b_grp, w_exp, b_exp, w_gate, w_up, w_down):
    B_, L, D = h.shape
    T = B_ * L
    xf = h.reshape(T, D)
    grp_prob = jax.nn.softmax((xf @ w_grp).astype(jnp.float32) + b_grp, axis=-1)
    grp_w, grp_idx = lax.top_k(grp_prob, 1)
    exp_logits = ((xf @ w_exp).astype(jnp.float32) + b_exp).reshape(T, N_GROUPS, EXPERTS_PER_GROUP)
    exp_logits = jnp.take_along_axis(exp_logits, grp_idx[:, :, None], axis=1)[:, 0]
    top_p, top_i = lax.top_k(jax.nn.softmax(exp_logits, axis=-1), TOP_K)
    gate = grp_w * (top_p / top_p.sum(-1, keepdims=True))
    expert = grp_idx * EXPERTS_PER_GROUP + top_i
    A = T * TOP_K
    e_flat = expert.reshape(A)
    tok = jnp.repeat(jnp.arange(T), TOP_K)
    order = jnp.argsort(e_flat)
    e_sorted = e_flat[order]
    tok_sorted = tok[order]
    g_sorted = gate.reshape(A)[order]
    counts = jnp.bincount(e_flat, length=N_EXPERTS)
    padded = ((counts + MOE_BLOCK - 1) // MOE_BLOCK) * MOE_BLOCK
    start = jnp.cumsum(counts) - counts
    pend = jnp.cumsum(padded)
    pstart = pend - padded
    pos = pstart[e_sorted] + jnp.arange(A) - start[e_sorted]
    n_blocks = (A + MOE_BLOCK - 1) // MOE_BLOCK + N_EXPERTS
    n_rows = n_blocks * MOE_BLOCK
    x_rows = jnp.zeros((n_rows, D), h.dtype).at[pos].set(xf[tok_sorted])
    block_expert = jnp.minimum(
        jnp.searchsorted(pend, jnp.arange(n_blocks) * MOE_BLOCK, side="right"), N_EXPERTS - 1)

    def expert_block(args):
        xb, e = args
        hid = jax.nn.silu(xb @ w_gate[e]) * (xb @ w_up[e])
        return hid @ w_down[e]

    y_rows = lax.map(expert_block, (x_rows.reshape(n_blocks, MOE_BLOCK, D), block_expert)).reshape(n_rows, D)
    y = jnp.zeros((T, D), jnp.float32).at[tok_sorted].add(y_rows[pos].astype(jnp.float32) * g_sorted[:, None])
    return y.reshape(B_, L, D).astype(h.dtype)


def setup_inputs(seed: int = 0) -> dict:
    key = jax.random.key(seed)
    ks = jax.random.split(key, 24)
    f32 = jnp.float32
    nrm = lambda k, shape, s: jax.random.normal(k, shape, f32) * s
    dt = jnp.exp(jax.random.uniform(ks[10], (DEPTH, DN_HEADS), f32, math.log(1e-3), math.log(1e-1)))
    return {
        "x": nrm(ks[0], (BATCH, SEQ, D_MODEL), 1.0),
        "c": nrm(ks[1], (BATCH, D_MODEL), 1.0),
        "w_ada": nrm(ks[2], (DEPTH, D_MODEL, 6 * D_MODEL), D_MODEL ** -0.5),
        "b_ada": nrm(ks[3], (DEPTH, 6 * D_MODEL), 0.01),
        "w_in": nrm(ks[4], (DEPTH, D_MODEL, D_IN), D_MODEL ** -0.5),
        "conv_w": nrm(ks[5], (DEPTH, CONV_WIDTH, D_CONV), CONV_WIDTH ** -0.5),
        "conv_norm_w": 1.0 + nrm(ks[6], (DEPTH, D_CONV), 0.01),
        "dn_conv_w": nrm(ks[7], (DEPTH, DN_CONV_WIDTH, 3 * D_DELTA), DN_CONV_WIDTH ** -0.5),
        "dn_A_log": jnp.log(jax.random.uniform(ks[8], (DEPTH, DN_HEADS), f32, 1.0, 16.0)),
        "dn_dt_bias": dt + jnp.log(-jnp.expm1(-dt)),
        "dn_norm_w": 1.0 + nrm(ks[9], (DEPTH, DN_HEAD_DIM), 0.01),
        "w_out": nrm(ks[11], (DEPTH, D_MIX, D_MODEL), D_MIX ** -0.5 * DEEP_BETA),
        "ln1_g": 1.0 + nrm(ks[12], (DEPTH, D_MODEL), 0.01),
        "ln1_b": nrm(ks[13], (DEPTH, D_MODEL), 0.01),
        "w_grp": nrm(ks[14], (DEPTH, D_MODEL, N_GROUPS), D_MODEL ** -0.5),
        "b_grp": nrm(ks[15], (DEPTH, N_GROUPS), 0.01),
        "w_exp": nrm(ks[16], (DEPTH, D_MODEL, N_EXPERTS), D_MODEL ** -0.5),
        "b_exp": nrm(ks[17], (DEPTH, N_EXPERTS), 0.01),
        "w_gate": nrm(ks[18], (DEPTH, N_EXPERTS, D_MODEL, D_EXPERT), D_MODEL ** -0.5),
        "w_up": nrm(ks[19], (DEPTH, N_EXPERTS, D_MODEL, D_EXPERT), D_MODEL ** -0.5),
        "w_down": nrm(ks[20], (DEPTH, N_EXPERTS, D_EXPERT, D_MODEL), D_EXPERT ** -0.5 * DEEP_BETA),
        "ln2_g": 1.0 + nrm(ks[21], (DEPTH, D_MODEL), 0.01),
        "ln2_b": nrm(ks[22], (DEPTH, D_MODEL), 0.01),
    }


def reference(x, c, w_ada, b_ada, w_in, conv_w, conv_norm_w, dn_conv_w, dn_A_log, dn_dt_bias, dn_norm_w,
              w_out, ln1_g, ln1_b, w_grp, b_grp, w_exp, b_exp, w_gate, w_up, w_down, ln2_g, ln2_b):
    split_at = [int(i) for i in np.cumsum(IN_SPLITS)[:-1]]
    c_act = jax.nn.silu(c)
    for l in range(DEPTH):
        mod = c_act @ w_ada[l] + b_ada[l]
        shift1, scale1, gate1, shift2, scale2, gate2 = [m[:, None, :] for m in jnp.split(mod, 6, axis=-1)]
        h = x * (1.0 + scale1) + shift1
        proj = h @ w_in[l]
        b_gate, c_gate, u, qkv, z, beta_logit, a_logit = jnp.split(proj, split_at, axis=-1)
        y_conv = short_conv_mixer(b_gate, c_gate, u, conv_w[l], conv_norm_w[l])
        y_dn = gated_deltanet(qkv, z, beta_logit, a_logit, dn_conv_w[l], dn_A_log[l], dn_dt_bias[l], dn_norm_w[l])
        mix = jnp.concatenate([y_conv, y_dn], axis=-1) @ w_out[l]
        x = layer_norm(DEEP_ALPHA * x + gate1 * mix, ln1_g[l], ln1_b[l])
        h = x * (1.0 + scale2) + shift2
        ffn = hierarchical_moe(h, w_grp[l], b_grp[l], w_exp[l], b_exp[l], w_gate[l], w_up[l], w_down[l])
        x = layer_norm(DEEP_ALPHA * x + gate2 * ffn, ln2_g[l], ln2_b[l])
    return x
```

```python
import functools

import numpy as np
import jax
import jax.numpy as jnp
from jax import lax
from jax.experimental import pallas as pl
from jax.experimental.pallas import tpu as pltpu

F32 = jnp.float32
BF16 = jnp.bfloat16

D_MODEL = 1024
D_CONV = 512
CONV_GROUP = 64
DN_HEADS = 4
DN_HEAD_DIM = 128
D_DELTA = DN_HEADS * DN_HEAD_DIM
N_GROUPS = 4
EXPERTS_PER_GROUP = 8
N_EXPERTS = N_GROUPS * EXPERTS_PER_GROUP
TOP_K = 2
D_EXPERT = 256
LN_EPS = 1e-5
RMS_EPS = 1e-6
DEEP_ALPHA = 2.0 ** 0.25

LANES = 128
HALO = 8
SEQ_TILE = 256
CHUNK = 128
N_LEVELS = 7
ROUTE_TILE = 1024
ROW_BLOCK = 256
FINAL_TILE = 512
VMEM_LIMIT = 56 * 1024 * 1024


def _dot(a, b):
    return jnp.dot(a, b, preferred_element_type=F32)


def _dot_nt(a, b):
    return lax.dot_general(a, b, (((1,), (1,)), ((), ())), preferred_element_type=F32)


def _split(x):
    hi = x.astype(BF16)
    lo = (x - hi.astype(F32)).astype(BF16)
    return hi, lo


def _dot_split_lhs(x, m):
    hi, lo = _split(x)
    return _dot(hi, m) + _dot(lo, m)


def _dot_split_rhs(m, x):
    hi, lo = _split(x)
    return _dot(m, hi) + _dot(m, lo)


def _sigmoid(x):
    return 1.0 / (1.0 + jnp.exp(-x))


def _silu(x):
    return x * _sigmoid(x)


def _ada_kernel(c_ref, w_ref, b_ref, o_ref):
    c = c_ref[...]
    o_ref[...] = _dot(_silu(c).astype(BF16), w_ref[...].astype(BF16)) + b_ref[...]


def _ada(c, w, b):
    n_b, d = c.shape
    n_out = w.shape[1]
    return pl.pallas_call(
        _ada_kernel,
        grid=(n_out // d,),
        in_specs=[
            pl.BlockSpec((n_b, d), lambda i: (0, 0)),
            pl.BlockSpec((d, d), lambda i: (0, i)),
            pl.BlockSpec((1, d), lambda i: (0, i)),
        ],
        out_specs=pl.BlockSpec((n_b, d), lambda i: (0, i)),
        out_shape=jax.ShapeDtypeStruct((n_b, n_out), F32),
        name="ada",
    )(c, w, b)


def _shifted(full, k):
    return pltpu.roll(full, k, axis=0)[HALO:]


def _mixer_kernel(x_ref, mod_ref, w_main_ref, w_ba_ref, conv_w_ref, conv_nw_ref, dn_conv_w_ref,
                  a_row_ref, dt_row_ref, dn_nw_ref, w_out_ref, ln_g_ref, ln_b_ref, w_rt_ref, b_rt_ref,
                  g64_ref, g128_ref, tri_ref, e_beta_ref, e_g_ref, lvl_ref,
                  x1_ref, h2_ref, logit_ref,
                  cu_buf, qkv_buf, s_ref, q_s, k_s, v_s, o_s, beta_s, gc_s):
    ts = x_ref.shape[1]

    @pl.when(pl.program_id(1) == 0)
    def _():
        cu_buf[0:HALO, :] = jnp.zeros((HALO, D_CONV), F32)
        qkv_buf[0:HALO, :] = jnp.zeros((HALO, 3 * D_DELTA), F32)
        s_ref[...] = jnp.zeros(s_ref.shape, F32)

    x = x_ref[0]
    shift1 = mod_ref[0, 0:1, :]
    scale1 = mod_ref[0, 1:2, :]
    gate1 = mod_ref[0, 2:3, :]
    shift2 = mod_ref[0, 3:4, :]
    scale2 = mod_ref[0, 4:5, :]
    h = (x * (1.0 + scale1) + shift1).astype(BF16)

    b_gate = _dot(h, w_main_ref[:, 0:D_CONV])
    c_gate = _dot(h, w_main_ref[:, D_CONV:2 * D_CONV])
    u_in = _dot(h, w_main_ref[:, 2 * D_CONV:3 * D_CONV])
    cu = c_gate * u_in
    cu_buf[HALO:HALO + ts, :] = cu
    full = cu_buf[...]
    cw = conv_w_ref[...]
    conv = cw[0:1, :] * _shifted(full, 2) + cw[1:2, :] * _shifted(full, 1) + cw[2:3, :] * cu
    cu_buf[0:HALO, :] = cu[ts - HALO:ts, :]
    y = b_gate * conv
    ms = _dot_split_lhs(y * y, g64_ref[...]) * (1.0 / CONV_GROUP)
    y_conv = y * lax.rsqrt(ms + RMS_EPS) * conv_nw_ref[...]

    off = 3 * D_CONV
    qkv = _dot(h, w_main_ref[:, off:off + 3 * D_DELTA])
    qkv_buf[HALO:HALO + ts, :] = qkv
    fullq = qkv_buf[...]
    w4 = dn_conv_w_ref[...]
    acc = (w4[0:1, :] * _shifted(fullq, 3) + w4[1:2, :] * _shifted(fullq, 2)
           + w4[2:3, :] * _shifted(fullq, 1) + w4[3:4, :] * qkv)
    qkv_buf[0:HALO, :] = qkv[ts - HALO:ts, :]
    act = _silu(acc)
    q = act[:, 0:D_DELTA]
    k = act[:, D_DELTA:2 * D_DELTA]
    g128 = g128_ref[...]
    q_s[...] = q * lax.rsqrt(_dot_split_lhs(q * q, g128) + RMS_EPS) * (DN_HEAD_DIM ** -0.5)
    k_s[...] = k * lax.rsqrt(_dot_split_lhs(k * k, g128) + RMS_EPS)
    v_s[...] = act[:, 2 * D_DELTA:3 * D_DELTA]

    ba = _dot(h, w_ba_ref[...])
    beta_all = _sigmoid(ba)
    a_in = ba + dt_row_ref[...]
    softplus = jnp.maximum(a_in, 0.0) + jnp.log1p(jnp.exp(-jnp.abs(a_in)))
    g_all = -jnp.exp(a_row_ref[...]) * softplus
    gc_all = _dot_split_rhs(tri_ref[...], g_all)
    beta_s[...] = _dot_split_lhs(beta_all, e_beta_ref[...])
    gc_s[...] = _dot_split_lhs(gc_all, e_g_ref[...])

    row = lax.broadcasted_iota(jnp.int32, (CHUNK, CHUNK), 0)
    col = lax.broadcasted_iota(jnp.int32, (CHUNK, CHUNK), 1)
    incl = row >= col
    eye = (row == col).astype(F32)
    for c in range(ts // CHUNK):
        rows = slice(c * CHUNK, (c + 1) * CHUNK)
        for hd in range(DN_HEADS):
            cols = slice(hd * DN_HEAD_DIM, (hd + 1) * DN_HEAD_DIM)
            qh = q_s[rows, cols]
            kh = k_s[rows, cols]
            vh = v_s[rows, cols]
            beta = beta_s[rows, cols]
            gc = gc_s[rows, cols]
            gam = jnp.exp(jnp.where(incl, gc - gc.T, -jnp.inf))
            kb = kh.astype(BF16)
            kk = _dot_nt(kb, kb)
            qk = _dot_nt(qh.astype(BF16), kb)
            lmat = beta * kk * gam
            t_inv = eye - lmat * lvl_ref[0]
            for lv in range(1, N_LEVELS):
                cb = (lmat * lvl_ref[lv]).astype(BF16)
                tb = t_inv.astype(BF16)
                t_inv = t_inv - _dot(tb, _dot(cb, tb).astype(BF16))
            eg = jnp.exp(gc)
            rhs = jnp.concatenate([beta * vh, beta * eg * kh], axis=1).astype(BF16)
            sol = _dot(t_inv.astype(BF16), rhs)
            u_c = sol[:, 0:DN_HEAD_DIM]
            w_c = sol[:, DN_HEAD_DIM:2 * DN_HEAD_DIM]
            state = s_ref[hd]
            sb = state.astype(BF16)
            delta = u_c - _dot(w_c.astype(BF16), sb)
            db = delta.astype(BF16)
            o_s[rows, cols] = _dot((qh * eg).astype(BF16), sb) + _dot((qk * gam).astype(BF16), db)
            g_last = gc[CHUNK - 1:CHUNK, :]
            k_end = kh * jnp.exp(g_last - gc)
            s_ref[hd] = state * jnp.exp(g_last) + _dot(k_end.T.astype(BF16), db)

    o = o_s[...]
    z = _dot(h, w_main_ref[:, off + 3 * D_DELTA:off + 4 * D_DELTA])
    o_ms = _dot_split_lhs(o * o, g128) * (1.0 / DN_HEAD_DIM)
    y_dn = o * lax.rsqrt(o_ms + RMS_EPS) * dn_nw_ref[...] * _silu(z)
    mix_in = jnp.concatenate([y_conv, y_dn], axis=1).astype(BF16)
    mix = _dot(mix_in, w_out_ref[...])
    r = DEEP_ALPHA * x + gate1 * mix
    mu = jnp.mean(r, axis=-1, keepdims=True)
    rc = r - mu
    var = jnp.mean(rc * rc, axis=-1, keepdims=True)
    x1 = rc * lax.rsqrt(var + LN_EPS) * ln_g_ref[...] + ln_b_ref[...]
    x1_ref[0] = x1
    h2 = x1 * (1.0 + scale2) + shift2
    h2_ref[0] = h2
    logit_ref[0] = _dot(h2.astype(BF16), w_rt_ref[...]) + b_rt_ref[...]


def _const_spec(shape):
    nd = len(shape)
    return pl.BlockSpec(shape, lambda b, j, _nd=nd: (0,) * _nd, pipeline_mode=pl.Buffered(1))


def _mixer_constants(ts):
    i = np.arange(4 * DN_HEAD_DIM)
    g64 = (i[:, None] // CONV_GROUP == i[None, :] // CONV_GROUP).astype(np.float32)
    g128 = (i[:, None] // DN_HEAD_DIM == i[None, :] // DN_HEAD_DIM).astype(np.float32)
    t = np.arange(ts)
    tri = ((t[:, None] // CHUNK == t[None, :] // CHUNK) & (t[None, :] <= t[:, None])).astype(np.float32)
    lane = np.arange(LANES)
    e_beta = (lane[:, None] == i[None, :] // DN_HEAD_DIM).astype(np.float32)
    e_g = (lane[:, None] == DN_HEADS + i[None, :] // DN_HEAD_DIM).astype(np.float32)
    r = np.arange(CHUNK)
    lvl = np.stack([(((r[:, None] >> l) ^ (r[None, :] >> l)) == 1) & (r[:, None] > r[None, :])
                    for l in range(N_LEVELS)]).astype(np.float32)
    as_bf16 = lambda a: jnp.asarray(a, dtype=BF16)
    return as_bf16(g64), as_bf16(g128), as_bf16(tri), as_bf16(e_beta), as_bf16(e_g), jnp.asarray(lvl)


def _mixer(x, mod, w_in, conv_w, conv_norm_w, dn_conv_w, dn_a_log, dn_dt_bias, dn_norm_w, w_out,
           ln_g, ln_b, w_grp, b_grp, w_exp, b_exp):
    n_b, seq, d = x.shape
    ts = SEQ_TILE
    n_main = 3 * D_CONV + 4 * D_DELTA
    w_main = w_in[:, :n_main].astype(BF16)
    w_ba = jnp.pad(w_in[:, n_main:], ((0, 0), (0, LANES - 2 * DN_HEADS))).astype(BF16)
    head_pad = (DN_HEADS, LANES - 2 * DN_HEADS)
    a_row = jnp.pad(dn_a_log, head_pad).reshape(1, LANES)
    dt_row = jnp.pad(dn_dt_bias, head_pad).reshape(1, LANES)
    n_rt = N_GROUPS + N_EXPERTS
    w_rt = jnp.pad(jnp.concatenate([w_grp, w_exp], axis=1), ((0, 0), (0, LANES - n_rt))).astype(BF16)
    b_rt = jnp.pad(jnp.concatenate([b_grp, b_exp]), (0, LANES - n_rt)).reshape(1, LANES)
    consts = _mixer_constants(ts)
    operands = (
        x, mod, w_main, w_ba, conv_w, conv_norm_w.reshape(1, D_CONV), dn_conv_w, a_row, dt_row,
        jnp.tile(dn_norm_w, DN_HEADS).reshape(1, D_DELTA), w_out.astype(BF16),
        ln_g.reshape(1, d), ln_b.reshape(1, d), w_rt, b_rt) + consts
    in_specs = [
        pl.BlockSpec((1, ts, d), lambda b, j: (b, j, 0)),
        pl.BlockSpec((1,) + mod.shape[1:], lambda b, j: (b, 0, 0)),
    ] + [_const_spec(a.shape) for a in operands[2:]]
    tile_spec = lambda width: pl.BlockSpec((1, ts, width), lambda b, j: (b, j, 0))
    return pl.pallas_call(
        _mixer_kernel,
        grid=(n_b, seq // ts),
        in_specs=in_specs,
        out_specs=[tile_spec(d), tile_spec(d), tile_spec(LANES)],
        out_shape=[jax.ShapeDtypeStruct((n_b, seq, d), F32), jax.ShapeDtypeStruct((n_b, seq, d), F32),
                   jax.ShapeDtypeStruct((n_b, seq, LANES), F32)],
        scratch_shapes=[
            pltpu.VMEM((HALO + ts, D_CONV), F32),
            pltpu.VMEM((HALO + ts, 3 * D_DELTA), F32),
            pltpu.VMEM((DN_HEADS, DN_HEAD_DIM, DN_HEAD_DIM), F32),
        ] + [pltpu.VMEM((ts, D_DELTA), F32)] * 6,
        compiler_params=pltpu.CompilerParams(dimension_semantics=("arbitrary", "arbitrary"),
                                             vmem_limit_bytes=VMEM_LIMIT),
        name="mixer",
    )(*operands)


def _route_kernel(lg_ref, o_ref):
    lg = lg_ref[...]
    lane = lax.broadcasted_iota(jnp.int32, lg.shape, 1).astype(F32)
    neg = -jnp.inf
    big = float(LANES)
    is_grp = lane < N_GROUPS
    gl = jnp.where(is_grp, lg, neg)
    gm = jnp.max(gl, axis=-1, keepdims=True)
    gi = jnp.min(jnp.where(gl == gm, lane, big), axis=-1, keepdims=True)
    grp_w = 1.0 / jnp.sum(jnp.where(is_grp, jnp.exp(lg - gm), 0.0), axis=-1, keepdims=True)
    lo = N_GROUPS + gi * EXPERTS_PER_GROUP
    el = jnp.where((lane >= lo) & (lane < lo + EXPERTS_PER_GROUP), lg, neg)
    m1 = jnp.max(el, axis=-1, keepdims=True)
    i1 = jnp.min(jnp.where(el == m1, lane, big), axis=-1, keepdims=True)
    el2 = jnp.where(lane == i1, neg, el)
    m2 = jnp.max(el2, axis=-1, keepdims=True)
    i2 = jnp.min(jnp.where(el2 == m2, lane, big), axis=-1, keepdims=True)
    ratio = jnp.exp(m2 - m1)
    g0 = grp_w / (1.0 + ratio)
    g1 = g0 * ratio
    o_ref[...] = jnp.where(lane == 0, i1 - N_GROUPS,
                           jnp.where(lane == 1, i2 - N_GROUPS,
                                     jnp.where(lane == 2, g0, jnp.where(lane == 3, g1, 0.0))))


def _route(logits):
    n_tok = logits.shape[0]
    spec = pl.BlockSpec((ROUTE_TILE, LANES), lambda i: (i, 0))
    return pl.pallas_call(
        _route_kernel,
        grid=(n_tok // ROUTE_TILE,),
        in_specs=[spec],
        out_specs=spec,
        out_shape=jax.ShapeDtypeStruct((n_tok, LANES), F32),
        name="route",
    )(logits)


def _expert_kernel(be_ref, nv_ref, src_ref, dst_ref, h2_hbm, wg_ref, wu_ref, wd_ref, y2_hbm,
                   xbuf, ybuf, sem_in, sem_out):
    del be_ref
    n_rows = xbuf.shape[0]
    n_real = y2_hbm.shape[0] - n_rows

    def row_in(r, tok):
        return pltpu.make_async_copy(h2_hbm.at[pl.ds(tok, 1)], xbuf.at[pl.ds(r, 1)], sem_in)

    def row_out(r, dst):
        return pltpu.make_async_copy(ybuf.at[pl.ds(r, 1)], y2_hbm.at[pl.ds(dst, 1)], sem_out)

    @pl.when(pl.program_id(0) == 0)
    def _():
        xbuf[...] = jnp.zeros(xbuf.shape, F32)
        pad_init = pltpu.make_async_copy(xbuf, y2_hbm.at[pl.ds(n_real, n_rows)], sem_out)
        pad_init.start()
        pad_init.wait()

    @pl.when(nv_ref[pl.program_id(0)] > 0)
    def _():
        def start_in(r, carry):
            row_in(r, src_ref[0, 0, r]).start()
            return carry

        def wait_in(r, carry):
            row_in(r, 0).wait()
            return carry

        lax.fori_loop(0, n_rows, start_in, 0)
        lax.fori_loop(0, n_rows, wait_in, 0)
        xb = xbuf[...].astype(BF16)
        gate = _dot(xb, wg_ref[0])
        up = _dot(xb, wu_ref[0])
        ybuf[...] = _dot((_silu(gate) * up).astype(BF16), wd_ref[0])

        def start_out(r, carry):
            row_out(r, dst_ref[0, 0, r]).start()
            return carry

        def wait_out(r, carry):
            row_out(r, 0).wait()
            return carry

        lax.fori_loop(0, n_rows, start_out, 0)
        lax.fori_loop(0, n_rows, wait_out, 0)


def _experts(h2, block_expert, n_valid, src_tok, dst_row, w_gate, w_up, w_down, n_out_rows):
    n_blocks = block_expert.shape[0]
    d = h2.shape[1]
    idx_spec = pl.BlockSpec((1, 1, ROW_BLOCK), lambda i, be, nv: (i, 0, 0), memory_space=pltpu.SMEM)
    grid_spec = pltpu.PrefetchScalarGridSpec(
        num_scalar_prefetch=2,
        grid=(n_blocks,),
        in_specs=[
            idx_spec, idx_spec,
            pl.BlockSpec(memory_space=pl.ANY),
            pl.BlockSpec((1, d, D_EXPERT), lambda i, be, nv: (be[i], 0, 0)),
            pl.BlockSpec((1, d, D_EXPERT), lambda i, be, nv: (be[i], 0, 0)),
            pl.BlockSpec((1, D_EXPERT, d), lambda i, be, nv: (be[i], 0, 0)),
        ],
        out_specs=pl.BlockSpec(memory_space=pl.ANY),
        scratch_shapes=[
            pltpu.VMEM((ROW_BLOCK, d), F32),
            pltpu.VMEM((ROW_BLOCK, d), F32),
            pltpu.SemaphoreType.DMA(()),
            pltpu.SemaphoreType.DMA(()),
        ],
    )
    return pl.pallas_call(
        _expert_kernel,
        grid_spec=grid_spec,
        out_shape=jax.ShapeDtypeStruct((n_out_rows, d), F32),
        compiler_params=pltpu.CompilerParams(dimension_semantics=("arbitrary",)),
        name="experts",
    )(block_expert, n_valid, src_tok.reshape(n_blocks, 1, ROW_BLOCK), dst_row.reshape(n_blocks, 1, ROW_BLOCK),
      h2, w_gate.astype(BF16), w_up.astype(BF16), w_down.astype(BF16))


def _row_layout(expert_ids, n_tok):
    n_asg = n_tok * TOP_K
    e_flat = expert_ids.reshape(n_asg)
    onehot = (e_flat[:, None] == jnp.arange(N_EXPERTS, dtype=jnp.int32)[None, :]).astype(jnp.int32)
    csum = jnp.cumsum(onehot, axis=0)
    rank = jnp.take_along_axis(csum, e_flat[:, None], axis=1)[:, 0] - 1
    counts = csum[-1]
    padded = ((counts + ROW_BLOCK - 1) // ROW_BLOCK) * ROW_BLOCK
    pend = jnp.cumsum(padded)
    pstart = pend - padded
    pos = pstart[e_flat] + rank
    n_blocks = n_asg // ROW_BLOCK + N_EXPERTS
    block_start = jnp.arange(n_blocks, dtype=jnp.int32) * ROW_BLOCK
    block_expert = jnp.minimum(jnp.searchsorted(pend, block_start, side="right"),
                               N_EXPERTS - 1).astype(jnp.int32)
    n_valid = jnp.clip((pstart + counts)[block_expert] - block_start, 0, ROW_BLOCK).astype(jnp.int32)
    asg = jnp.arange(n_asg, dtype=jnp.int32)
    r = jnp.arange(n_blocks * ROW_BLOCK, dtype=jnp.int32)
    src_tok = jnp.zeros_like(r).at[pos].set(asg // TOP_K)
    dst_row = (n_asg + r % ROW_BLOCK).at[pos].set((asg % TOP_K) * n_tok + asg // TOP_K)
    return block_expert, n_valid, src_tok, dst_row, n_asg + ROW_BLOCK


def _final_kernel(x1_ref, mod_ref, y0_ref, y1_ref, route_ref, ln_g_ref, ln_b_ref, o_ref):
    x1 = x1_ref[0]
    gate2 = mod_ref[0, 5:6, :]
    rt = route_ref[...]
    ffn = rt[:, 2:3] * y0_ref[...] + rt[:, 3:4] * y1_ref[...]
    r = DEEP_ALPHA * x1 + gate2 * ffn
    mu = jnp.mean(r, axis=-1, keepdims=True)
    rc = r - mu
    var = jnp.mean(rc * rc, axis=-1, keepdims=True)
    o_ref[0] = rc * lax.rsqrt(var + LN_EPS) * ln_g_ref[...] + ln_b_ref[...]


def _final(x1, mod, y2, route, ln_g, ln_b):
    n_b, seq, d = x1.shape
    tt = FINAL_TILE
    per_b = seq // tt
    n_tok_blocks = n_b * per_b
    return pl.pallas_call(
        _final_kernel,
        grid=(n_b, per_b),
        in_specs=[
            pl.BlockSpec((1, tt, d), lambda b, j: (b, j, 0)),
            pl.BlockSpec((1,) + mod.shape[1:], lambda b, j: (b, 0, 0)),
            pl.BlockSpec((tt, d), lambda b, j: (b * per_b + j, 0)),
            pl.BlockSpec((tt, d), lambda b, j: (n_tok_blocks + b * per_b + j, 0)),
            pl.BlockSpec((tt, LANES), lambda b, j: (b * per_b + j, 0)),
            pl.BlockSpec((1, d), lambda b, j: (0, 0)),
            pl.BlockSpec((1, d), lambda b, j: (0, 0)),
        ],
        out_specs=pl.BlockSpec((1, tt, d), lambda b, j: (b, j, 0)),
        out_shape=jax.ShapeDtypeStruct((n_b, seq, d), F32),
        name="final",
    )(x1, mod, y2, y2, route, ln_g.reshape(1, d), ln_b.reshape(1, d))


def kernel(x, c, w_ada, b_ada, w_in, conv_w, conv_norm_w, dn_conv_w, dn_A_log, dn_dt_bias, dn_norm_w,
           w_out, ln1_g, ln1_b, w_grp, b_grp, w_exp, b_exp, w_gate, w_up, w_down, ln2_g, ln2_b):
    n_b, seq, d = x.shape
    n_tok = n_b * seq
    for l in range(w_ada.shape[0]):
        mod = _ada(c, w_ada[l], b_ada[l].reshape(1, -1)).reshape(n_b, 6, d)
        x1, h2, logits = _mixer(x, mod, w_in[l], conv_w[l], conv_norm_w[l], dn_conv_w[l], dn_A_log[l],
                                dn_dt_bias[l], dn_norm_w[l], w_out[l], ln1_g[l], ln1_b[l],
                                w_grp[l], b_grp[l], w_exp[l], b_exp[l])
        route = _route(logits.reshape(n_tok, LANES))
        block_expert, n_valid, src_tok, dst_row, n_out_rows = _row_layout(
            route[:, 0:TOP_K].astype(jnp.int32), n_tok)
        y2 = _experts(h2.reshape(n_tok, d), block_expert, n_valid, src_tok, dst_row,
                      w_gate[l], w_up[l], w_down[l], n_out_rows)
        x = _final(x1, mod, y2, route, ln2_g[l], ln2_b[l])
    return x
```

```python
import functools

import numpy as np
import jax
import jax.numpy as jnp
from jax import lax
from jax.experimental import pallas as pl
from jax.experimental.pallas import tpu as pltpu

F32 = jnp.float32
BF16 = jnp.bfloat16

D_MODEL = 1024
D_CONV = 512
CONV_GROUP = 64
DN_HEADS = 4
DN_HEAD_DIM = 128
D_DELTA = DN_HEADS * DN_HEAD_DIM
N_GROUPS = 4
EXPERTS_PER_GROUP = 8
N_EXPERTS = N_GROUPS * EXPERTS_PER_GROUP
TOP_K = 2
D_EXPERT = 256
LN_EPS = 1e-5
RMS_EPS = 1e-6
DEEP_ALPHA = 2.0 ** 0.25

LANES = 128
HALO = 8
SEQ_TILE = 256
CHUNK = 128
N_LEVELS = 7
ROUTE_TILE = 1024
ROW_BLOCK = 256
ROW_LINES = D_MODEL // LANES
DMA_UNROLL = 8
FINAL_TILE = 512
VMEM_LIMIT = 56 * 1024 * 1024


def _dot(a, b):
    return jnp.dot(a, b, preferred_element_type=F32)


def _dot_nt(a, b):
    return lax.dot_general(a, b, (((1,), (1,)), ((), ())), preferred_element_type=F32)


def _split(x):
    hi = x.astype(BF16)
    lo = (x - hi.astype(F32)).astype(BF16)
    return hi, lo


def _dot_split_lhs(x, m):
    hi, lo = _split(x)
    return _dot(hi, m) + _dot(lo, m)


def _dot_split_rhs(m, x):
    hi, lo = _split(x)
    return _dot(m, hi) + _dot(m, lo)


def _sigmoid(x):
    return 1.0 / (1.0 + jnp.exp(-x))


def _silu(x):
    return x * _sigmoid(x)


def _ada_kernel(c_ref, w_ref, b_ref, o_ref):
    c = c_ref[...]
    o_ref[...] = _dot(_silu(c).astype(BF16), w_ref[...].astype(BF16)) + b_ref[...]


def _ada(c, w, b):
    n_b, d = c.shape
    n_out = w.shape[1]
    return pl.pallas_call(
        _ada_kernel,
        grid=(n_out // d,),
        in_specs=[
            pl.BlockSpec((n_b, d), lambda i: (0, 0)),
            pl.BlockSpec((d, d), lambda i: (0, i)),
            pl.BlockSpec((1, d), lambda i: (0, i)),
        ],
        out_specs=pl.BlockSpec((n_b, d), lambda i: (0, i)),
        out_shape=jax.ShapeDtypeStruct((n_b, n_out), F32),
        name="ada",
    )(c, w, b)


def _shifted(full, k):
    return pltpu.roll(full, k, axis=0)[HALO:]


def _mixer_kernel(x_ref, mod_ref, w_main_ref, w_ba_ref, conv_w_ref, conv_nw_ref, dn_conv_w_ref,
                  a_row_ref, dt_row_ref, dn_nw_ref, w_out_ref, ln_g_ref, ln_b_ref, w_rt_ref, b_rt_ref,
                  g64_ref, g128_ref, tri_ref, e_beta_ref, e_g_ref, lvl_ref,
                  x1_ref, h2_ref, logit_ref,
                  cu_buf, qkv_buf, s_ref, q_s, k_s, v_s, o_s, beta_s, gc_s, u_s,
                  w_s, qkg_s, qd_s, kend_t_s):
    ts = x_ref.shape[1]

    @pl.when(pl.program_id(1) == 0)
    def _():
        cu_buf[0:HALO, :] = jnp.zeros((HALO, D_CONV), F32)
        qkv_buf[0:HALO, :] = jnp.zeros((HALO, 3 * D_DELTA), F32)
        s_ref[...] = jnp.zeros(s_ref.shape, F32)

    x = x_ref[0]
    shift1 = mod_ref[0, 0:1, :]
    scale1 = mod_ref[0, 1:2, :]
    gate1 = mod_ref[0, 2:3, :]
    shift2 = mod_ref[0, 3:4, :]
    scale2 = mod_ref[0, 4:5, :]
    h = (x * (1.0 + scale1) + shift1).astype(BF16)

    b_gate = _dot(h, w_main_ref[:, 0:D_CONV])
    c_gate = _dot(h, w_main_ref[:, D_CONV:2 * D_CONV])
    u_in = _dot(h, w_main_ref[:, 2 * D_CONV:3 * D_CONV])
    cu = c_gate * u_in
    cu_buf[HALO:HALO + ts, :] = cu
    full = cu_buf[...]
    cw = conv_w_ref[...]
    conv = cw[0:1, :] * _shifted(full, 2) + cw[1:2, :] * _shifted(full, 1) + cw[2:3, :] * cu
    cu_buf[0:HALO, :] = cu[ts - HALO:ts, :]
    y = b_gate * conv
    ms = _dot_split_lhs(y * y, g64_ref[...]) * (1.0 / CONV_GROUP)
    y_conv = y * lax.rsqrt(ms + RMS_EPS) * conv_nw_ref[...]

    off = 3 * D_CONV
    qkv = _dot(h, w_main_ref[:, off:off + 3 * D_DELTA])
    qkv_buf[HALO:HALO + ts, :] = qkv
    fullq = qkv_buf[...]
    w4 = dn_conv_w_ref[...]
    acc = (w4[0:1, :] * _shifted(fullq, 3) + w4[1:2, :] * _shifted(fullq, 2)
           + w4[2:3, :] * _shifted(fullq, 1) + w4[3:4, :] * qkv)
    qkv_buf[0:HALO, :] = qkv[ts - HALO:ts, :]
    act = _silu(acc)
    q = act[:, 0:D_DELTA]
    k = act[:, D_DELTA:2 * D_DELTA]
    g128 = g128_ref[...]
    q_s[...] = q * lax.rsqrt(_dot_split_lhs(q * q, g128) + RMS_EPS) * (DN_HEAD_DIM ** -0.5)
    k_s[...] = k * lax.rsqrt(_dot_split_lhs(k * k, g128) + RMS_EPS)
    v_s[...] = act[:, 2 * D_DELTA:3 * D_DELTA]

    ba = _dot(h, w_ba_ref[...])
    beta_all = _sigmoid(ba)
    a_in = ba + dt_row_ref[...]
    softplus = jnp.maximum(a_in, 0.0) + jnp.log1p(jnp.exp(-jnp.abs(a_in)))
    g_all = -jnp.exp(a_row_ref[...]) * softplus
    gc_all = _dot_split_rhs(tri_ref[...], g_all)
    beta_s[...] = _dot_split_lhs(beta_all, e_beta_ref[...])
    gc_s[...] = _dot_split_lhs(gc_all, e_g_ref[...])

    row = lax.broadcasted_iota(jnp.int32, (CHUNK, CHUNK), 0)
    col = lax.broadcasted_iota(jnp.int32, (CHUNK, CHUNK), 1)
    incl = row >= col
    eye = (row == col).astype(F32)
    blocks = [(slice(c * CHUNK, (c + 1) * CHUNK), slice(hd * DN_HEAD_DIM, (hd + 1) * DN_HEAD_DIM))
              for c in range(ts // CHUNK) for hd in range(DN_HEADS)]

    lmats, t_invs = [], []
    for rows, cols in blocks:
        qh = q_s[rows, cols]
        kh = k_s[rows, cols]
        gc = gc_s[rows, cols]
        gam = jnp.exp(jnp.where(incl, gc - gc.T, -jnp.inf))
        kb = kh.astype(BF16)
        lmat = beta_s[rows, cols] * _dot_nt(kb, kb) * gam
        qkg_s[rows, cols] = (_dot_nt(qh.astype(BF16), kb) * gam).astype(BF16)
        eg = jnp.exp(gc)
        qd_s[rows, cols] = (qh * eg).astype(BF16)
        k_end = kh * jnp.exp(gc[CHUNK - 1:CHUNK, :] - gc)
        kend_t_s[rows, cols] = k_end.T.astype(BF16)
        lmats.append(lmat)
        t_invs.append(eye - lmat * lvl_ref[0])
    for lv in range(1, N_LEVELS):
        tbs = [t.astype(BF16) for t in t_invs]
        ys = [_dot((lm * lvl_ref[lv]).astype(BF16), tb).astype(BF16) for lm, tb in zip(lmats, tbs)]
        t_invs = [t - _dot(tb, y) for t, tb, y in zip(t_invs, tbs, ys)]
    for (rows, cols), t_inv in zip(blocks, t_invs):
        beta = beta_s[rows, cols]
        kh = k_s[rows, cols]
        rhs = jnp.concatenate([beta * v_s[rows, cols], beta * jnp.exp(gc_s[rows, cols]) * kh], axis=1)
        sol = _dot(t_inv.astype(BF16), rhs.astype(BF16))
        u_s[rows, cols] = sol[:, 0:DN_HEAD_DIM]
        w_s[rows, cols] = sol[:, DN_HEAD_DIM:2 * DN_HEAD_DIM].astype(BF16)

    heads = [slice(hd * DN_HEAD_DIM, (hd + 1) * DN_HEAD_DIM) for hd in range(DN_HEADS)]
    for c in range(ts // CHUNK):
        rows = slice(c * CHUNK, (c + 1) * CHUNK)
        states = [s_ref[hd] for hd in range(DN_HEADS)]
        sbs = [st.astype(BF16) for st in states]
        dbs = [(u_s[rows, cols] - _dot(w_s[rows, cols], sb)).astype(BF16) for cols, sb in zip(heads, sbs)]
        for hd, cols in enumerate(heads):
            o_s[rows, cols] = _dot(qd_s[rows, cols], sbs[hd]) + _dot(qkg_s[rows, cols], dbs[hd])
            decay = jnp.exp(gc_s[(c + 1) * CHUNK - 1:(c + 1) * CHUNK, cols])
            s_ref[hd] = states[hd] * decay + _dot(kend_t_s[rows, cols], dbs[hd])

    o = o_s[...]
    z = _dot(h, w_main_ref[:, off + 3 * D_DELTA:off + 4 * D_DELTA])
    o_ms = _dot_split_lhs(o * o, g128) * (1.0 / DN_HEAD_DIM)
    y_dn = o * lax.rsqrt(o_ms + RMS_EPS) * dn_nw_ref[...] * _silu(z)
    mix_in = jnp.concatenate([y_conv, y_dn], axis=1).astype(BF16)
    mix = _dot(mix_in, w_out_ref[...])
    r = DEEP_ALPHA * x + gate1 * mix
    mu = jnp.mean(r, axis=-1, keepdims=True)
    rc = r - mu
    var = jnp.mean(rc * rc, axis=-1, keepdims=True)
    x1 = rc * lax.rsqrt(var + LN_EPS) * ln_g_ref[...] + ln_b_ref[...]
    x1_ref[0] = x1
    h2 = x1 * (1.0 + scale2) + shift2
    _store_rows(h2_ref.at[0], h2)
    logit_ref[0] = _dot(h2.astype(BF16), w_rt_ref[...]) + b_rt_ref[...]


def _const_spec(shape):
    nd = len(shape)
    return pl.BlockSpec(shape, lambda b, j, _nd=nd: (0,) * _nd, pipeline_mode=pl.Buffered(1))


def _mixer_constants(ts):
    i = np.arange(4 * DN_HEAD_DIM)
    g64 = (i[:, None] // CONV_GROUP == i[None, :] // CONV_GROUP).astype(np.float32)
    g128 = (i[:, None] // DN_HEAD_DIM == i[None, :] // DN_HEAD_DIM).astype(np.float32)
    t = np.arange(ts)
    tri = ((t[:, None] // CHUNK == t[None, :] // CHUNK) & (t[None, :] <= t[:, None])).astype(np.float32)
    lane = np.arange(LANES)
    e_beta = (lane[:, None] == i[None, :] // DN_HEAD_DIM).astype(np.float32)
    e_g = (lane[:, None] == DN_HEADS + i[None, :] // DN_HEAD_DIM).astype(np.float32)
    r = np.arange(CHUNK)
    lvl = np.stack([(((r[:, None] >> l) ^ (r[None, :] >> l)) == 1) & (r[:, None] > r[None, :])
                    for l in range(N_LEVELS)]).astype(np.float32)
    as_bf16 = lambda a: jnp.asarray(a, dtype=BF16)
    return as_bf16(g64), as_bf16(g128), as_bf16(tri), as_bf16(e_beta), as_bf16(e_g), jnp.asarray(lvl)


def _mixer(x, mod, w_in, conv_w, conv_norm_w, dn_conv_w, dn_a_log, dn_dt_bias, dn_norm_w, w_out,
           ln_g, ln_b, w_grp, b_grp, w_exp, b_exp):
    n_b, seq, d = x.shape
    ts = SEQ_TILE
    n_main = 3 * D_CONV + 4 * D_DELTA
    w_main = w_in[:, :n_main].astype(BF16)
    w_ba = jnp.pad(w_in[:, n_main:], ((0, 0), (0, LANES - 2 * DN_HEADS))).astype(BF16)
    head_pad = (DN_HEADS, LANES - 2 * DN_HEADS)
    a_row = jnp.pad(dn_a_log, head_pad).reshape(1, LANES)
    dt_row = jnp.pad(dn_dt_bias, head_pad).reshape(1, LANES)
    n_rt = N_GROUPS + N_EXPERTS
    w_rt = jnp.pad(jnp.concatenate([w_grp, w_exp], axis=1), ((0, 0), (0, LANES - n_rt))).astype(BF16)
    b_rt = jnp.pad(jnp.concatenate([b_grp, b_exp]), (0, LANES - n_rt)).reshape(1, LANES)
    consts = _mixer_constants(ts)
    operands = (
        x, mod, w_main, w_ba, conv_w, conv_norm_w.reshape(1, D_CONV), dn_conv_w, a_row, dt_row,
        jnp.tile(dn_norm_w, DN_HEADS).reshape(1, D_DELTA), w_out.astype(BF16),
        ln_g.reshape(1, d), ln_b.reshape(1, d), w_rt, b_rt) + consts
    in_specs = [
        pl.BlockSpec((1, ts, d), lambda b, j: (b, j, 0)),
        pl.BlockSpec((1,) + mod.shape[1:], lambda b, j: (b, 0, 0)),
    ] + [_const_spec(a.shape) for a in operands[2:]]
    tile_spec = lambda width: pl.BlockSpec((1, ts, width), lambda b, j: (b, j, 0))
    return pl.pallas_call(
        _mixer_kernel,
        grid=(n_b, seq // ts),
        in_specs=in_specs,
        out_specs=[tile_spec(d), pl.BlockSpec((1, ts * ROW_LINES, LANES), lambda b, j: (b, j, 0)),
                   tile_spec(LANES)],
        out_shape=[jax.ShapeDtypeStruct((n_b, seq, d), F32),
                   jax.ShapeDtypeStruct((n_b, seq * ROW_LINES, LANES), F32),
                   jax.ShapeDtypeStruct((n_b, seq, LANES), F32)],
        scratch_shapes=[
            pltpu.VMEM((HALO + ts, D_CONV), F32),
            pltpu.VMEM((HALO + ts, 3 * D_DELTA), F32),
            pltpu.VMEM((DN_HEADS, DN_HEAD_DIM, DN_HEAD_DIM), F32),
        ] + [pltpu.VMEM((ts, D_DELTA), F32)] * 7 + [pltpu.VMEM((ts, D_DELTA), BF16)] * 4,
        compiler_params=pltpu.CompilerParams(dimension_semantics=("arbitrary", "arbitrary"),
                                             vmem_limit_bytes=VMEM_LIMIT),
        name="mixer",
    )(*operands)


def _route_kernel(lg_ref, o_ref):
    lg = lg_ref[...]
    lane = lax.broadcasted_iota(jnp.int32, lg.shape, 1).astype(F32)
    neg = -jnp.inf
    big = float(LANES)
    is_grp = lane < N_GROUPS
    gl = jnp.where(is_grp, lg, neg)
    gm = jnp.max(gl, axis=-1, keepdims=True)
    gi = jnp.min(jnp.where(gl == gm, lane, big), axis=-1, keepdims=True)
    grp_w = 1.0 / jnp.sum(jnp.where(is_grp, jnp.exp(lg - gm), 0.0), axis=-1, keepdims=True)
    lo = N_GROUPS + gi * EXPERTS_PER_GROUP
    el = jnp.where((lane >= lo) & (lane < lo + EXPERTS_PER_GROUP), lg, neg)
    m1 = jnp.max(el, axis=-1, keepdims=True)
    i1 = jnp.min(jnp.where(el == m1, lane, big), axis=-1, keepdims=True)
    el2 = jnp.where(lane == i1, neg, el)
    m2 = jnp.max(el2, axis=-1, keepdims=True)
    i2 = jnp.min(jnp.where(el2 == m2, lane, big), axis=-1, keepdims=True)
    ratio = jnp.exp(m2 - m1)
    g0 = grp_w / (1.0 + ratio)
    g1 = g0 * ratio
    o_ref[...] = jnp.where(lane == 0, i1 - N_GROUPS,
                           jnp.where(lane == 1, i2 - N_GROUPS,
                                     jnp.where(lane == 2, g0, jnp.where(lane == 3, g1, 0.0))))


def _route(logits):
    n_tok = logits.shape[0]
    spec = pl.BlockSpec((ROUTE_TILE, LANES), lambda i: (i, 0))
    return pl.pallas_call(
        _route_kernel,
        grid=(n_tok // ROUTE_TILE,),
        in_specs=[spec],
        out_specs=spec,
        out_shape=jax.ShapeDtypeStruct((n_tok, LANES), F32),
        name="route",
    )(logits)


def _load_rows(ref, n_rows):
    return jnp.concatenate([ref[pl.ds(s, n_rows, stride=ROW_LINES), :] for s in range(ROW_LINES)], axis=1)


def _store_rows(ref, val):
    for s in range(ROW_LINES):
        ref[pl.ds(s, val.shape[0], stride=ROW_LINES), :] = val[:, s * LANES:(s + 1) * LANES]


def _expert_kernel(be_ref, nv_ref, src_ref, src_next_ref, dst_ref, h2_hbm, wg_ref, wu_ref, wd_ref, y2_hbm,
                   xbuf, ybuf, sem_in, sem_out):
    del be_ref
    i = pl.program_id(0)
    last = pl.num_programs(0) - 1
    slot = i % 2
    block_lines = xbuf.shape[1]
    n_rows = block_lines // ROW_LINES
    pad_line0 = y2_hbm.shape[0] - block_lines

    def start_gather(idx_ref, to_slot):
        def group(g, carry):
            for j in range(DMA_UNROLL):
                r = g * DMA_UNROLL + j
                line = pl.multiple_of(idx_ref[0, 0, r] * ROW_LINES, ROW_LINES)
                pltpu.make_async_copy(
                    h2_hbm.at[pl.ds(line, ROW_LINES)],
                    xbuf.at[to_slot, pl.ds(pl.multiple_of(r * ROW_LINES, ROW_LINES), ROW_LINES)],
                    sem_in.at[to_slot]).start(priority=j % 2)
            return carry
        lax.fori_loop(0, n_rows // DMA_UNROLL, group, 0)

    def start_scatter(from_slot):
        def group(g, carry):
            for j in range(DMA_UNROLL):
                r = g * DMA_UNROLL + j
                line = pl.multiple_of(dst_ref[0, 0, r] * ROW_LINES, ROW_LINES)
                pltpu.make_async_copy(
                    ybuf.at[from_slot, pl.ds(pl.multiple_of(r * ROW_LINES, ROW_LINES), ROW_LINES)],
                    y2_hbm.at[pl.ds(line, ROW_LINES)],
                    sem_out.at[from_slot]).start(priority=j % 2)
            return carry
        lax.fori_loop(0, n_rows // DMA_UNROLL, group, 0)

    def wait_gather(at_slot):
        pltpu.make_async_copy(h2_hbm.at[pl.ds(0, block_lines)], xbuf.at[at_slot], sem_in.at[at_slot]).wait()

    def wait_scatter(at_slot):
        pltpu.make_async_copy(ybuf.at[at_slot], y2_hbm.at[pl.ds(0, block_lines)], sem_out.at[at_slot]).wait()

    @pl.when(i == 0)
    def _():
        xbuf[...] = jnp.zeros(xbuf.shape, F32)
        pad_init = pltpu.make_async_copy(xbuf.at[0], y2_hbm.at[pl.ds(pad_line0, block_lines)], sem_out.at[0])
        pad_init.start()
        pad_init.wait()

        @pl.when(nv_ref[0] > 0)
        def _():
            start_gather(src_ref, 0)

    @pl.when((i < last) & (nv_ref[jnp.minimum(i + 1, last)] > 0))
    def _():
        start_gather(src_next_ref, 1 - slot)

    @pl.when((i >= 2) & (nv_ref[jnp.maximum(i - 2, 0)] > 0))
    def _():
        wait_scatter(slot)

    @pl.when(nv_ref[i] > 0)
    def _():
        wait_gather(slot)
        xb = _load_rows(xbuf.at[slot], n_rows).astype(BF16)
        gate = _dot(xb, wg_ref[0])
        up = _dot(xb, wu_ref[0])
        _store_rows(ybuf.at[slot], _dot((_silu(gate) * up).astype(BF16), wd_ref[0]))
        start_scatter(slot)

    @pl.when(i == last)
    def _():
        @pl.when(nv_ref[i] > 0)
        def _():
            wait_scatter(slot)

        @pl.when((i >= 1) & (nv_ref[jnp.maximum(i - 1, 0)] > 0))
        def _():
            wait_scatter(1 - slot)


def _experts(h2_lines, block_expert, n_valid, src_tok, dst_row, w_gate, w_up, w_down, n_out_rows):
    n_blocks = block_expert.shape[0]
    d = w_gate.shape[1]
    block_lines = ROW_BLOCK * ROW_LINES
    src3 = src_tok.reshape(n_blocks, 1, ROW_BLOCK)
    idx_spec = pl.BlockSpec((1, 1, ROW_BLOCK), lambda i, be, nv: (i, 0, 0), memory_space=pltpu.SMEM)
    next_spec = pl.BlockSpec((1, 1, ROW_BLOCK), lambda i, be, nv: (jnp.minimum(i + 1, n_blocks - 1), 0, 0),
                             memory_space=pltpu.SMEM)
    grid_spec = pltpu.PrefetchScalarGridSpec(
        num_scalar_prefetch=2,
        grid=(n_blocks,),
        in_specs=[
            idx_spec, next_spec, idx_spec,
            pl.BlockSpec(memory_space=pl.ANY),
            pl.BlockSpec((1, d, D_EXPERT), lambda i, be, nv: (be[i], 0, 0)),
            pl.BlockSpec((1, d, D_EXPERT), lambda i, be, nv: (be[i], 0, 0)),
            pl.BlockSpec((1, D_EXPERT, d), lambda i, be, nv: (be[i], 0, 0)),
        ],
        out_specs=pl.BlockSpec(memory_space=pl.ANY),
        scratch_shapes=[
            pltpu.VMEM((2, block_lines, LANES), F32),
            pltpu.VMEM((2, block_lines, LANES), F32),
            pltpu.SemaphoreType.DMA((2,)),
            pltpu.SemaphoreType.DMA((2,)),
        ],
    )
    return pl.pallas_call(
        _expert_kernel,
        grid_spec=grid_spec,
        out_shape=jax.ShapeDtypeStruct((n_out_rows * ROW_LINES, LANES), F32),
        compiler_params=pltpu.CompilerParams(dimension_semantics=("arbitrary",)),
        name="experts",
    )(block_expert, n_valid, src3, src3, dst_row.reshape(n_blocks, 1, ROW_BLOCK),
      h2_lines, w_gate.astype(BF16), w_up.astype(BF16), w_down.astype(BF16))


def _row_layout(expert_ids, n_tok):
    n_asg = n_tok * TOP_K
    e_flat = expert_ids.reshape(n_asg)
    onehot = (e_flat[:, None] == jnp.arange(N_EXPERTS, dtype=jnp.int32)[None, :]).astype(jnp.int32)
    csum = jnp.cumsum(onehot, axis=0)
    rank = jnp.take_along_axis(csum, e_flat[:, None], axis=1)[:, 0] - 1
    counts = csum[-1]
    padded = ((counts + ROW_BLOCK - 1) // ROW_BLOCK) * ROW_BLOCK
    pend = jnp.cumsum(padded)
    pstart = pend - padded
    pos = pstart[e_flat] + rank
    n_blocks = n_asg // ROW_BLOCK + N_EXPERTS
    block_start = jnp.arange(n_blocks, dtype=jnp.int32) * ROW_BLOCK
    block_expert = jnp.minimum(jnp.sum(pend[None, :] <= block_start[:, None], axis=1),
                               N_EXPERTS - 1).astype(jnp.int32)
    n_valid = jnp.clip((pstart + counts)[block_expert] - block_start, 0, ROW_BLOCK).astype(jnp.int32)
    r = jnp.arange(n_blocks * ROW_BLOCK, dtype=jnp.int32)
    asg = jnp.full_like(r, -1).at[pos].set(jnp.arange(n_asg, dtype=jnp.int32))
    real = asg >= 0
    src_tok = jnp.where(real, asg // TOP_K, 0)
    dst_row = jnp.where(real, (asg % TOP_K) * n_tok + asg // TOP_K, n_asg + r % ROW_BLOCK)
    return block_expert, n_valid, src_tok, dst_row, n_asg + ROW_BLOCK


def _final_kernel(x1_ref, mod_ref, y0_ref, y1_ref, route_ref, ln_g_ref, ln_b_ref, o_ref):
    x1 = x1_ref[0]
    tt = x1.shape[0]
    gate2 = mod_ref[0, 5:6, :]
    rt = route_ref[...]
    ffn = rt[:, 2:3] * _load_rows(y0_ref, tt) + rt[:, 3:4] * _load_rows(y1_ref, tt)
    r = DEEP_ALPHA * x1 + gate2 * ffn
    mu = jnp.mean(r, axis=-1, keepdims=True)
    rc = r - mu
    var = jnp.mean(rc * rc, axis=-1, keepdims=True)
    o_ref[0] = rc * lax.rsqrt(var + LN_EPS) * ln_g_ref[...] + ln_b_ref[...]


def _final(x1, mod, y2, route, ln_g, ln_b):
    n_b, seq, d = x1.shape
    tt = FINAL_TILE
    per_b = seq // tt
    n_tok_blocks = n_b * per_b
    return pl.pallas_call(
        _final_kernel,
        grid=(n_b, per_b),
        in_specs=[
            pl.BlockSpec((1, tt, d), lambda b, j: (b, j, 0)),
            pl.BlockSpec((1,) + mod.shape[1:], lambda b, j: (b, 0, 0)),
            pl.BlockSpec((tt * ROW_LINES, LANES), lambda b, j: (b * per_b + j, 0)),
            pl.BlockSpec((tt * ROW_LINES, LANES), lambda b, j: (n_tok_blocks + b * per_b + j, 0)),
            pl.BlockSpec((tt, LANES), lambda b, j: (b * per_b + j, 0)),
            pl.BlockSpec((1, d), lambda b, j: (0, 0)),
            pl.BlockSpec((1, d), lambda b, j: (0, 0)),
        ],
        out_specs=pl.BlockSpec((1, tt, d), lambda b, j: (b, j, 0)),
        out_shape=jax.ShapeDtypeStruct((n_b, seq, d), F32),
        name="final",
    )(x1, mod, y2, y2, route, ln_g.reshape(1, d), ln_b.reshape(1, d))


def kernel(x, c, w_ada, b_ada, w_in, conv_w, conv_norm_w, dn_conv_w, dn_A_log, dn_dt_bias, dn_norm_w,
           w_out, ln1_g, ln1_b, w_grp, b_grp, w_exp, b_exp, w_gate, w_up, w_down, ln2_g, ln2_b):
    n_b, seq, d = x.shape
    n_tok = n_b * seq
    for l in range(w_ada.shape[0]):
        mod = _ada(c, w_ada[l], b_ada[l].reshape(1, -1)).reshape(n_b, 6, d)
        x1, h2, logits = _mixer(x, mod, w_in[l], conv_w[l], conv_norm_w[l], dn_conv_w[l], dn_A_log[l],
                                dn_dt_bias[l], dn_norm_w[l], w_out[l], ln1_g[l], ln1_b[l],
                                w_grp[l], b_grp[l], w_exp[l], b_exp[l])
        route = _route(logits.reshape(n_tok, LANES))
        block_expert, n_valid, src_tok, dst_row, n_out_rows = _row_layout(
            route[:, 0:TOP_K].astype(jnp.int32), n_tok)
        y2 = _experts(h2.reshape(n_tok * ROW_LINES, LANES), block_expert, n_valid, src_tok, dst_row,
                      w_gate[l], w_up[l], w_down[l], n_out_rows)
        x = _final(x1, mod, y2, route, ln2_g[l], ln2_b[l])
    return x
```

```python
import functools

import numpy as np
import jax
import jax.numpy as jnp
from jax import lax
from jax.experimental import pallas as pl
from jax.experimental.pallas import tpu as pltpu

F32 = jnp.float32
BF16 = jnp.bfloat16

D_MODEL = 1024
D_CONV = 512
CONV_GROUP = 64
DN_HEADS = 4
DN_HEAD_DIM = 128
D_DELTA = DN_HEADS * DN_HEAD_DIM
N_GROUPS = 4
EXPERTS_PER_GROUP = 8
N_EXPERTS = N_GROUPS * EXPERTS_PER_GROUP
TOP_K = 2
D_EXPERT = 256
LN_EPS = 1e-5
RMS_EPS = 1e-6
DEEP_ALPHA = 2.0 ** 0.25

LANES = 128
MXU_TILE = 256
HALO = 8
SEQ_TILE = 512
CHUNK = 128
N_LEVELS = 7
ROUTE_TILE = 1024
ROW_BLOCK = 256
ROW_LINES = D_MODEL // LANES
DMA_UNROLL = 8
FINAL_TILE = 512
VMEM_LIMIT = 56 * 1024 * 1024


def _dot(a, b):
    return jnp.dot(a, b, preferred_element_type=F32)


def _dot_nt(a, b):
    return lax.dot_general(a, b, (((1,), (1,)), ((), ())), preferred_element_type=F32)


def _split(x):
    hi = x.astype(BF16)
    lo = (x - hi.astype(F32)).astype(BF16)
    return hi, lo


def _dot_split_lhs(x, m):
    hi, lo = _split(x)
    return _dot(hi, m) + _dot(lo, m)


def _group_sums(x, g):
    w = g.shape[0]
    return jnp.concatenate([_dot_split_lhs(x[:, i:i + w], g) for i in range(0, x.shape[1], w)], axis=1)


def _dot_split_rhs(m, x):
    hi, lo = _split(x)
    return _dot(m, hi) + _dot(m, lo)


def _sigmoid(x):
    return 1.0 / (1.0 + jnp.exp(-x))


def _silu(x):
    return x * _sigmoid(x)


def _ada_kernel(c_ref, w_ref, b_ref, o_ref):
    c = c_ref[...]
    o_ref[...] = _dot(_silu(c).astype(BF16), w_ref[...].astype(BF16)) + b_ref[...]


def _ada(c, w, b):
    n_b, d = c.shape
    n_out = w.shape[1]
    return pl.pallas_call(
        _ada_kernel,
        grid=(n_out // d,),
        in_specs=[
            pl.BlockSpec((n_b, d), lambda i: (0, 0)),
            pl.BlockSpec((d, d), lambda i: (0, i)),
            pl.BlockSpec((1, d), lambda i: (0, i)),
        ],
        out_specs=pl.BlockSpec((n_b, d), lambda i: (0, i)),
        out_shape=jax.ShapeDtypeStruct((n_b, n_out), F32),
        name="ada",
    )(c, w, b)


def _shifted(full, k):
    return pltpu.roll(full, k, axis=0)[HALO:]


def _mixer_kernel(x_ref, mod_ref, w_main_ref, w_ba_ref, conv_w_ref, conv_nw_ref, dn_conv_w_ref,
                  a_row_ref, dt_row_ref, dn_nw_ref, w_out_ref, ln_g_ref, ln_b_ref, w_rt_ref, b_rt_ref,
                  g64_ref, g128_ref, tri_ref, e_beta_ref, e_g_ref, lvl_ref,
                  x1_ref, h2_ref, logit_ref,
                  cu_buf, qkv_buf, s_ref, q_s, k_s, v_s, o_s, beta_s, gc_s, u_s,
                  w_s, qkg_s, qd_s, kend_t_s):
    ts = x_ref.shape[1]

    @pl.when(pl.program_id(1) == 0)
    def _():
        cu_buf[0:HALO, :] = jnp.zeros((HALO, D_CONV), F32)
        qkv_buf[0:HALO, :] = jnp.zeros((HALO, 3 * D_DELTA), F32)
        s_ref[...] = jnp.zeros(s_ref.shape, F32)

    x = x_ref[0]
    shift1 = mod_ref[0, 0:1, :]
    scale1 = mod_ref[0, 1:2, :]
    gate1 = mod_ref[0, 2:3, :]
    shift2 = mod_ref[0, 3:4, :]
    scale2 = mod_ref[0, 4:5, :]
    h = (x * (1.0 + scale1) + shift1).astype(BF16)

    b_gate = _dot(h, w_main_ref[:, 0:D_CONV])
    c_gate = _dot(h, w_main_ref[:, D_CONV:2 * D_CONV])
    u_in = _dot(h, w_main_ref[:, 2 * D_CONV:3 * D_CONV])
    cu = c_gate * u_in
    cu_buf[HALO:HALO + ts, :] = cu
    full = cu_buf[...]
    cw = conv_w_ref[...]
    conv = cw[0:1, :] * _shifted(full, 2) + cw[1:2, :] * _shifted(full, 1) + cw[2:3, :] * cu
    cu_buf[0:HALO, :] = cu[ts - HALO:ts, :]
    y = b_gate * conv
    ms = _group_sums(y * y, g64_ref[...]) * (1.0 / CONV_GROUP)
    y_conv = y * lax.rsqrt(ms + RMS_EPS) * conv_nw_ref[...]

    off = 3 * D_CONV
    qkv = _dot(h, w_main_ref[:, off:off + 3 * D_DELTA])
    qkv_buf[HALO:HALO + ts, :] = qkv
    fullq = qkv_buf[...]
    w4 = dn_conv_w_ref[...]
    acc = (w4[0:1, :] * _shifted(fullq, 3) + w4[1:2, :] * _shifted(fullq, 2)
           + w4[2:3, :] * _shifted(fullq, 1) + w4[3:4, :] * qkv)
    qkv_buf[0:HALO, :] = qkv[ts - HALO:ts, :]
    act = _silu(acc)
    q = act[:, 0:D_DELTA]
    k = act[:, D_DELTA:2 * D_DELTA]
    g128 = g128_ref[...]
    q_s[...] = q * lax.rsqrt(_group_sums(q * q, g128) + RMS_EPS) * (DN_HEAD_DIM ** -0.5)
    k_s[...] = k * lax.rsqrt(_group_sums(k * k, g128) + RMS_EPS)
    v_s[...] = act[:, 2 * D_DELTA:3 * D_DELTA]

    ba = _dot(h, w_ba_ref[...])
    beta_all = _sigmoid(ba)
    a_in = ba + dt_row_ref[...]
    softplus = jnp.maximum(a_in, 0.0) + jnp.log1p(jnp.exp(-jnp.abs(a_in)))
    g_all = -jnp.exp(a_row_ref[...]) * softplus
    gc_all = _dot_split_rhs(tri_ref[...], g_all)
    beta_s[...] = _dot_split_lhs(beta_all, e_beta_ref[...])
    gc_s[...] = _dot_split_lhs(gc_all, e_g_ref[...])

    row = lax.broadcasted_iota(jnp.int32, (CHUNK, CHUNK), 0)
    col = lax.broadcasted_iota(jnp.int32, (CHUNK, CHUNK), 1)
    incl = row >= col
    eye = (row == col).astype(F32)
    blocks = [(slice(c * CHUNK, (c + 1) * CHUNK), slice(hd * DN_HEAD_DIM, (hd + 1) * DN_HEAD_DIM))
              for c in range(ts // CHUNK) for hd in range(DN_HEADS)]

    lmats, t_invs = [], []
    for rows, cols in blocks:
        qh = q_s[rows, cols]
        kh = k_s[rows, cols]
        gc = gc_s[rows, cols]
        gam = jnp.exp(jnp.where(incl, gc - gc.T, -jnp.inf))
        kb = kh.astype(BF16)
        lmat = beta_s[rows, cols] * _dot_nt(kb, kb) * gam
        qkg_s[rows, cols] = (_dot_nt(qh.astype(BF16), kb) * gam).astype(BF16)
        eg = jnp.exp(gc)
        qd_s[rows, cols] = (qh * eg).astype(BF16)
        k_end = kh * jnp.exp(gc[CHUNK - 1:CHUNK, :] - gc)
        kend_t_s[rows, cols] = k_end.T.astype(BF16)
        lmats.append(lmat)
        t_invs.append(eye - lmat * lvl_ref[0])
    for lv in range(1, N_LEVELS):
        tbs = [t.astype(BF16) for t in t_invs]
        ys = [_dot((lm * lvl_ref[lv]).astype(BF16), tb).astype(BF16) for lm, tb in zip(lmats, tbs)]
        t_invs = [t - _dot(tb, y) for t, tb, y in zip(t_invs, tbs, ys)]
    for (rows, cols), t_inv in zip(blocks, t_invs):
        beta = beta_s[rows, cols]
        kh = k_s[rows, cols]
        rhs = jnp.concatenate([beta * v_s[rows, cols], beta * jnp.exp(gc_s[rows, cols]) * kh], axis=1)
        sol = _dot(t_inv.astype(BF16), rhs.astype(BF16))
        u_s[rows, cols] = sol[:, 0:DN_HEAD_DIM]
        w_s[rows, cols] = sol[:, DN_HEAD_DIM:2 * DN_HEAD_DIM].astype(BF16)

    heads = [slice(hd * DN_HEAD_DIM, (hd + 1) * DN_HEAD_DIM) for hd in range(DN_HEADS)]
    for c in range(ts // CHUNK):
        rows = slice(c * CHUNK, (c + 1) * CHUNK)
        states = [s_ref[hd] for hd in range(DN_HEADS)]
        sbs = [st.astype(BF16) for st in states]
        dbs = [(u_s[rows, cols] - _dot(w_s[rows, cols], sb)).astype(BF16) for cols, sb in zip(heads, sbs)]
        for hd, cols in enumerate(heads):
            o_s[rows, cols] = _dot(qd_s[rows, cols], sbs[hd]) + _dot(qkg_s[rows, cols], dbs[hd])
            decay = jnp.exp(gc_s[(c + 1) * CHUNK - 1:(c + 1) * CHUNK, cols])
            s_ref[hd] = states[hd] * decay + _dot(kend_t_s[rows, cols], dbs[hd])

    o = o_s[...]
    z = _dot(h, w_main_ref[:, off + 3 * D_DELTA:off + 4 * D_DELTA])
    o_ms = _group_sums(o * o, g128) * (1.0 / DN_HEAD_DIM)
    y_dn = o * lax.rsqrt(o_ms + RMS_EPS) * dn_nw_ref[...] * _silu(z)
    mix_in = jnp.concatenate([y_conv, y_dn], axis=1).astype(BF16)
    mix = _dot(mix_in, w_out_ref[...])
    r = DEEP_ALPHA * x + gate1 * mix
    mu = jnp.mean(r, axis=-1, keepdims=True)
    rc = r - mu
    var = jnp.mean(rc * rc, axis=-1, keepdims=True)
    x1 = rc * lax.rsqrt(var + LN_EPS) * ln_g_ref[...] + ln_b_ref[...]
    x1_ref[0] = x1
    h2 = x1 * (1.0 + scale2) + shift2
    _store_rows(h2_ref.at[0], h2)
    logit_ref[0] = _dot(h2.astype(BF16), w_rt_ref[...]) + b_rt_ref[...]


def _const_spec(shape):
    nd = len(shape)
    return pl.BlockSpec(shape, lambda b, j, _nd=nd: (0,) * _nd, pipeline_mode=pl.Buffered(1))


def _mixer_constants(ts):
    m = np.arange(MXU_TILE)
    g64 = (m[:, None] // CONV_GROUP == m[None, :] // CONV_GROUP).astype(np.float32)
    g128 = (m[:, None] // DN_HEAD_DIM == m[None, :] // DN_HEAD_DIM).astype(np.float32)
    i = np.arange(D_DELTA)
    t = np.arange(ts)
    tri = ((t[:, None] // CHUNK == t[None, :] // CHUNK) & (t[None, :] <= t[:, None])).astype(np.float32)
    lane = np.arange(LANES)
    e_beta = (lane[:, None] == i[None, :] // DN_HEAD_DIM).astype(np.float32)
    e_g = (lane[:, None] == DN_HEADS + i[None, :] // DN_HEAD_DIM).astype(np.float32)
    r = np.arange(CHUNK)
    lvl = np.stack([(((r[:, None] >> l) ^ (r[None, :] >> l)) == 1) & (r[:, None] > r[None, :])
                    for l in range(N_LEVELS)]).astype(np.float32)
    as_bf16 = lambda a: jnp.asarray(a, dtype=BF16)
    return as_bf16(g64), as_bf16(g128), as_bf16(tri), as_bf16(e_beta), as_bf16(e_g), jnp.asarray(lvl)


def _mixer(x, mod, w_in, conv_w, conv_norm_w, dn_conv_w, dn_a_log, dn_dt_bias, dn_norm_w, w_out,
           ln_g, ln_b, w_grp, b_grp, w_exp, b_exp):
    n_b, seq, d = x.shape
    ts = SEQ_TILE
    n_main = 3 * D_CONV + 4 * D_DELTA
    w_main = w_in[:, :n_main].astype(BF16)
    w_ba = jnp.pad(w_in[:, n_main:], ((0, 0), (0, LANES - 2 * DN_HEADS))).astype(BF16)
    head_pad = (DN_HEADS, LANES - 2 * DN_HEADS)
    a_row = jnp.pad(dn_a_log, head_pad).reshape(1, LANES)
    dt_row = jnp.pad(dn_dt_bias, head_pad).reshape(1, LANES)
    n_rt = N_GROUPS + N_EXPERTS
    w_rt = jnp.pad(jnp.concatenate([w_grp, w_exp], axis=1), ((0, 0), (0, LANES - n_rt))).astype(BF16)
    b_rt = jnp.pad(jnp.concatenate([b_grp, b_exp]), (0, LANES - n_rt)).reshape(1, LANES)
    consts = _mixer_constants(ts)
    operands = (
        x, mod, w_main, w_ba, conv_w, conv_norm_w.reshape(1, D_CONV), dn_conv_w, a_row, dt_row,
        jnp.tile(dn_norm_w, DN_HEADS).reshape(1, D_DELTA), w_out.astype(BF16),
        ln_g.reshape(1, d), ln_b.reshape(1, d), w_rt, b_rt) + consts
    in_specs = [
        pl.BlockSpec((1, ts, d), lambda b, j: (b, j, 0)),
        pl.BlockSpec((1,) + mod.shape[1:], lambda b, j: (b, 0, 0)),
    ] + [_const_spec(a.shape) for a in operands[2:]]
    tile_spec = lambda width: pl.BlockSpec((1, ts, width), lambda b, j: (b, j, 0))
    return pl.pallas_call(
        _mixer_kernel,
        grid=(n_b, seq // ts),
        in_specs=in_specs,
        out_specs=[tile_spec(d), pl.BlockSpec((1, ts * ROW_LINES, LANES), lambda b, j: (b, j, 0)),
                   tile_spec(LANES)],
        out_shape=[jax.ShapeDtypeStruct((n_b, seq, d), F32),
                   jax.ShapeDtypeStruct((n_b, seq * ROW_LINES, LANES), F32),
                   jax.ShapeDtypeStruct((n_b, seq, LANES), F32)],
        scratch_shapes=[
            pltpu.VMEM((HALO + ts, D_CONV), F32),
            pltpu.VMEM((HALO + ts, 3 * D_DELTA), F32),
            pltpu.VMEM((DN_HEADS, DN_HEAD_DIM, DN_HEAD_DIM), F32),
        ] + [pltpu.VMEM((ts, D_DELTA), F32)] * 7 + [pltpu.VMEM((ts, D_DELTA), BF16)] * 4,
        compiler_params=pltpu.CompilerParams(dimension_semantics=("arbitrary", "arbitrary"),
                                             vmem_limit_bytes=VMEM_LIMIT),
        name="mixer",
    )(*operands)


def _route_kernel(lg_ref, o_ref):
    lg = lg_ref[...]
    lane = lax.broadcasted_iota(jnp.int32, lg.shape, 1).astype(F32)
    neg = -jnp.inf
    big = float(LANES)
    is_grp = lane < N_GROUPS
    gl = jnp.where(is_grp, lg, neg)
    gm = jnp.max(gl, axis=-1, keepdims=True)
    gi = jnp.min(jnp.where(gl == gm, lane, big), axis=-1, keepdims=True)
    grp_w = 1.0 / jnp.sum(jnp.where(is_grp, jnp.exp(lg - gm), 0.0), axis=-1, keepdims=True)
    lo = N_GROUPS + gi * EXPERTS_PER_GROUP
    el = jnp.where((lane >= lo) & (lane < lo + EXPERTS_PER_GROUP), lg, neg)
    m1 = jnp.max(el, axis=-1, keepdims=True)
    i1 = jnp.min(jnp.where(el == m1, lane, big), axis=-1, keepdims=True)
    el2 = jnp.where(lane == i1, neg, el)
    m2 = jnp.max(el2, axis=-1, keepdims=True)
    i2 = jnp.min(jnp.where(el2 == m2, lane, big), axis=-1, keepdims=True)
    ratio = jnp.exp(m2 - m1)
    g0 = grp_w / (1.0 + ratio)
    g1 = g0 * ratio
    o_ref[...] = jnp.where(lane == 0, i1 - N_GROUPS,
                           jnp.where(lane == 1, i2 - N_GROUPS,
                                     jnp.where(lane == 2, g0, jnp.where(lane == 3, g1, 0.0))))


def _route(logits):
    n_tok = logits.shape[0]
    spec = pl.BlockSpec((ROUTE_TILE, LANES), lambda i: (i, 0))
    return pl.pallas_call(
        _route_kernel,
        grid=(n_tok // ROUTE_TILE,),
        in_specs=[spec],
        out_specs=spec,
        out_shape=jax.ShapeDtypeStruct((n_tok, LANES), F32),
        name="route",
    )(logits)


def _load_rows(ref, n_rows):
    return jnp.concatenate([ref[pl.ds(s, n_rows, stride=ROW_LINES), :] for s in range(ROW_LINES)], axis=1)


def _store_rows(ref, val):
    for s in range(ROW_LINES):
        ref[pl.ds(s, val.shape[0], stride=ROW_LINES), :] = val[:, s * LANES:(s + 1) * LANES]


def _expert_kernel(be_ref, nv_ref, src_ref, src_next_ref, dst_ref, h2_hbm, wg_ref, wu_ref, wd_ref, y2_hbm,
                   xbuf, ybuf, sem_in, sem_out):
    del be_ref
    i = pl.program_id(0)
    last = pl.num_programs(0) - 1
    slot = i % 2
    block_lines = xbuf.shape[1]
    n_rows = block_lines // ROW_LINES
    pad_line0 = y2_hbm.shape[0] - block_lines

    def start_gather(idx_ref, to_slot):
        def group(g, carry):
            for j in range(DMA_UNROLL):
                r = g * DMA_UNROLL + j
                line = pl.multiple_of(idx_ref[0, 0, r] * ROW_LINES, ROW_LINES)
                pltpu.make_async_copy(
                    h2_hbm.at[pl.ds(line, ROW_LINES)],
                    xbuf.at[to_slot, pl.ds(pl.multiple_of(r * ROW_LINES, ROW_LINES), ROW_LINES)],
                    sem_in.at[to_slot]).start(priority=j % 2)
            return carry
        lax.fori_loop(0, n_rows // DMA_UNROLL, group, 0)

    def start_scatter(from_slot):
        def group(g, carry):
            for j in range(DMA_UNROLL):
                r = g * DMA_UNROLL + j
                line = pl.multiple_of(dst_ref[0, 0, r] * ROW_LINES, ROW_LINES)
                pltpu.make_async_copy(
                    ybuf.at[from_slot, pl.ds(pl.multiple_of(r * ROW_LINES, ROW_LINES), ROW_LINES)],
                    y2_hbm.at[pl.ds(line, ROW_LINES)],
                    sem_out.at[from_slot]).start(priority=j % 2)
            return carry
        lax.fori_loop(0, n_rows // DMA_UNROLL, group, 0)

    def wait_gather(at_slot):
        pltpu.make_async_copy(h2_hbm.at[pl.ds(0, block_lines)], xbuf.at[at_slot], sem_in.at[at_slot]).wait()

    def wait_scatter(at_slot):
        pltpu.make_async_copy(ybuf.at[at_slot], y2_hbm.at[pl.ds(0, block_lines)], sem_out.at[at_slot]).wait()

    @pl.when(i == 0)
    def _():
        xbuf[...] = jnp.zeros(xbuf.shape, F32)
        pad_init = pltpu.make_async_copy(xbuf.at[0], y2_hbm.at[pl.ds(pad_line0, block_lines)], sem_out.at[0])
        pad_init.start()
        pad_init.wait()

        @pl.when(nv_ref[0] > 0)
        def _():
            start_gather(src_ref, 0)

    @pl.when((i < last) & (nv_ref[jnp.minimum(i + 1, last)] > 0))
    def _():
        start_gather(src_next_ref, 1 - slot)

    @pl.when((i >= 2) & (nv_ref[jnp.maximum(i - 2, 0)] > 0))
    def _():
        wait_scatter(slot)

    @pl.when(nv_ref[i] > 0)
    def _():
        wait_gather(slot)
        xb = _load_rows(xbuf.at[slot], n_rows).astype(BF16)
        gate = _dot(xb, wg_ref[0])
        up = _dot(xb, wu_ref[0])
        _store_rows(ybuf.at[slot], _dot((_silu(gate) * up).astype(BF16), wd_ref[0]))
        start_scatter(slot)

    @pl.when(i == last)
    def _():
        @pl.when(nv_ref[i] > 0)
        def _():
            wait_scatter(slot)

        @pl.when((i >= 1) & (nv_ref[jnp.maximum(i - 1, 0)] > 0))
        def _():
            wait_scatter(1 - slot)


def _experts(h2_lines, block_expert, n_valid, src_tok, dst_row, w_gate, w_up, w_down, n_out_rows):
    n_blocks = block_expert.shape[0]
    d = w_gate.shape[1]
    block_lines = ROW_BLOCK * ROW_LINES
    src3 = src_tok.reshape(n_blocks, 1, ROW_BLOCK)
    idx_spec = pl.BlockSpec((1, 1, ROW_BLOCK), lambda i, be, nv: (i, 0, 0), memory_space=pltpu.SMEM)
    next_spec = pl.BlockSpec((1, 1, ROW_BLOCK), lambda i, be, nv: (jnp.minimum(i + 1, n_blocks - 1), 0, 0),
                             memory_space=pltpu.SMEM)
    grid_spec = pltpu.PrefetchScalarGridSpec(
        num_scalar_prefetch=2,
        grid=(n_blocks,),
        in_specs=[
            idx_spec, next_spec, idx_spec,
            pl.BlockSpec(memory_space=pl.ANY),
            pl.BlockSpec((1, d, D_EXPERT), lambda i, be, nv: (be[i], 0, 0)),
            pl.BlockSpec((1, d, D_EXPERT), lambda i, be, nv: (be[i], 0, 0)),
            pl.BlockSpec((1, D_EXPERT, d), lambda i, be, nv: (be[i], 0, 0)),
        ],
        out_specs=pl.BlockSpec(memory_space=pl.ANY),
        scratch_shapes=[
            pltpu.VMEM((2, block_lines, LANES), F32),
            pltpu.VMEM((2, block_lines, LANES), F32),
            pltpu.SemaphoreType.DMA((2,)),
            pltpu.SemaphoreType.DMA((2,)),
        ],
    )
    return pl.pallas_call(
        _expert_kernel,
        grid_spec=grid_spec,
        out_shape=jax.ShapeDtypeStruct((n_out_rows * ROW_LINES, LANES), F32),
        compiler_params=pltpu.CompilerParams(dimension_semantics=("arbitrary",)),
        name="experts",
    )(block_expert, n_valid, src3, src3, dst_row.reshape(n_blocks, 1, ROW_BLOCK),
      h2_lines, w_gate.astype(BF16), w_up.astype(BF16), w_down.astype(BF16))


def _row_layout(expert_ids, n_tok):
    n_asg = n_tok * TOP_K
    e_flat = expert_ids.reshape(n_asg)
    onehot = (e_flat[:, None] == jnp.arange(N_EXPERTS, dtype=jnp.int32)[None, :]).astype(jnp.int32)
    csum = jnp.cumsum(onehot, axis=0)
    rank = jnp.take_along_axis(csum, e_flat[:, None], axis=1)[:, 0] - 1
    counts = csum[-1]
    padded = ((counts + ROW_BLOCK - 1) // ROW_BLOCK) * ROW_BLOCK
    pend = jnp.cumsum(padded)
    pstart = pend - padded
    pos = pstart[e_flat] + rank
    n_blocks = n_asg // ROW_BLOCK + N_EXPERTS
    block_start = jnp.arange(n_blocks, dtype=jnp.int32) * ROW_BLOCK
    block_expert = jnp.minimum(jnp.sum(pend[None, :] <= block_start[:, None], axis=1),
                               N_EXPERTS - 1).astype(jnp.int32)
    n_valid = jnp.clip((pstart + counts)[block_expert] - block_start, 0, ROW_BLOCK).astype(jnp.int32)
    r = jnp.arange(n_blocks * ROW_BLOCK, dtype=jnp.int32)
    asg = jnp.full_like(r, -1).at[pos].set(jnp.arange(n_asg, dtype=jnp.int32))
    real = asg >= 0
    src_tok = jnp.where(real, asg // TOP_K, 0)
    dst_row = jnp.where(real, (asg % TOP_K) * n_tok + asg // TOP_K, n_asg + r % ROW_BLOCK)
    return block_expert, n_valid, src_tok, dst_row, n_asg + ROW_BLOCK


def _final_kernel(x1_ref, mod_ref, y0_ref, y1_ref, route_ref, ln_g_ref, ln_b_ref, o_ref):
    x1 = x1_ref[0]
    tt = x1.shape[0]
    gate2 = mod_ref[0, 5:6, :]
    rt = route_ref[...]
    ffn = rt[:, 2:3] * _load_rows(y0_ref, tt) + rt[:, 3:4] * _load_rows(y1_ref, tt)
    r = DEEP_ALPHA * x1 + gate2 * ffn
    mu = jnp.mean(r, axis=-1, keepdims=True)
    rc = r - mu
    var = jnp.mean(rc * rc, axis=-1, keepdims=True)
    o_ref[0] = rc * lax.rsqrt(var + LN_EPS) * ln_g_ref[...] + ln_b_ref[...]


def _final(x1, mod, y2, route, ln_g, ln_b):
    n_b, seq, d = x1.shape
    tt = FINAL_TILE
    per_b = seq // tt
    n_tok_blocks = n_b * per_b
    return pl.pallas_call(
        _final_kernel,
        grid=(n_b, per_b),
        in_specs=[
            pl.BlockSpec((1, tt, d), lambda b, j: (b, j, 0)),
            pl.BlockSpec((1,) + mod.shape[1:], lambda b, j: (b, 0, 0)),
            pl.BlockSpec((tt * ROW_LINES, LANES), lambda b, j: (b * per_b + j, 0)),
            pl.BlockSpec((tt * ROW_LINES, LANES), lambda b, j: (n_tok_blocks + b * per_b + j, 0)),
            pl.BlockSpec((tt, LANES), lambda b, j: (b * per_b + j, 0)),
            pl.BlockSpec((1, d), lambda b, j: (0, 0)),
            pl.BlockSpec((1, d), lambda b, j: (0, 0)),
        ],
        out_specs=pl.BlockSpec((1, tt, d), lambda b, j: (b, j, 0)),
        out_shape=jax.ShapeDtypeStruct((n_b, seq, d), F32),
        name="final",
    )(x1, mod, y2, y2, route, ln_g.reshape(1, d), ln_b.reshape(1, d))


def kernel(x, c, w_ada, b_ada, w_in, conv_w, conv_norm_w, dn_conv_w, dn_A_log, dn_dt_bias, dn_norm_w,
           w_out, ln1_g, ln1_b, w_grp, b_grp, w_exp, b_exp, w_gate, w_up, w_down, ln2_g, ln2_b):
    n_b, seq, d = x.shape
    n_tok = n_b * seq
    for l in range(w_ada.shape[0]):
        mod = _ada(c, w_ada[l], b_ada[l].reshape(1, -1)).reshape(n_b, 6, d)
        x1, h2, logits = _mixer(x, mod, w_in[l], conv_w[l], conv_norm_w[l], dn_conv_w[l], dn_A_log[l],
                                dn_dt_bias[l], dn_norm_w[l], w_out[l], ln1_g[l], ln1_b[l],
                                w_grp[l], b_grp[l], w_exp[l], b_exp[l])
        route = _route(logits.reshape(n_tok, LANES))
        block_expert, n_valid, src_tok, dst_row, n_out_rows = _row_layout(
            route[:, 0:TOP_K].astype(jnp.int32), n_tok)
        y2 = _experts(h2.reshape(n_tok * ROW_LINES, LANES), block_expert, n_valid, src_tok, dst_row,
                      w_gate[l], w_up[l], w_down[l], n_out_rows)
        x = _final(x1, mod, y2, route, ln2_g[l], ln2_b[l])
    return x
```

```python
import functools

import numpy as np
import jax
import jax.numpy as jnp
from jax import lax
from jax.experimental import pallas as pl
from jax.experimental.pallas import tpu as pltpu

F32 = jnp.float32
BF16 = jnp.bfloat16

D_MODEL = 1024
D_CONV = 512
CONV_GROUP = 64
DN_HEADS = 4
DN_HEAD_DIM = 128
D_DELTA = DN_HEADS * DN_HEAD_DIM
N_GROUPS = 4
EXPERTS_PER_GROUP = 8
N_EXPERTS = N_GROUPS * EXPERTS_PER_GROUP
TOP_K = 2
D_EXPERT = 256
LN_EPS = 1e-5
RMS_EPS = 1e-6
DEEP_ALPHA = 2.0 ** 0.25

LANES = 128
MXU_TILE = 256
SUBLANES = 8
HALO = SUBLANES
SEQ_TILE = 512
CHUNK = 128
N_LEVELS = 7
ROUTE_TILE = 1024
ROW_BLOCK = 256
ROW_LINES = D_MODEL // LANES
DMA_UNROLL = 8
FINAL_TILE = 512
VMEM_LIMIT = 56 * 1024 * 1024


def _dot(a, b):
    return jnp.dot(a, b, preferred_element_type=F32)


def _dot_nt(a, b):
    return lax.dot_general(a, b, (((1,), (1,)), ((), ())), preferred_element_type=F32)


def _split(x):
    hi = x.astype(BF16)
    lo = (x - hi.astype(F32)).astype(BF16)
    return hi, lo


def _dot_split_lhs(x, m):
    hi, lo = _split(x)
    return _dot(hi, m) + _dot(lo, m)


def _group_sums(x, g):
    w = g.shape[0]
    return jnp.concatenate([_dot_split_lhs(x[:, i:i + w], g) for i in range(0, x.shape[1], w)], axis=1)


def _dot_split_rhs(m, x):
    hi, lo = _split(x)
    return _dot(m, hi) + _dot(m, lo)


def _sigmoid(x):
    return 1.0 / (1.0 + jnp.exp(-x))


def _silu(x):
    return x * _sigmoid(x)


def _load_rows(ref, n_rows):
    return jnp.concatenate([ref[pl.ds(s, n_rows, stride=ROW_LINES), :] for s in range(ROW_LINES)], axis=1)


def _store_rows(ref, val):
    for s in range(ROW_LINES):
        ref[pl.ds(s, val.shape[0], stride=ROW_LINES), :] = val[:, s * LANES:(s + 1) * LANES]


def _lines(row, n=1):
    return pl.ds(pl.multiple_of(row * ROW_LINES, ROW_LINES), n * ROW_LINES)


def _ada_kernel(c_ref, w_ref, b_ref, o_ref):
    c = c_ref[...]
    o_ref[...] = _dot(_silu(c).astype(BF16), w_ref[...].astype(BF16)) + b_ref[...]


def _ada(c, w, b):
    n_b, d = c.shape
    n_out = w.shape[1]
    return pl.pallas_call(
        _ada_kernel,
        grid=(n_out // d,),
        in_specs=[
            pl.BlockSpec((n_b, d), lambda i: (0, 0)),
            pl.BlockSpec((d, d), lambda i: (0, i)),
            pl.BlockSpec((1, d), lambda i: (0, i)),
        ],
        out_specs=pl.BlockSpec((n_b, d), lambda i: (0, i)),
        out_shape=jax.ShapeDtypeStruct((n_b, n_out), F32),
        name="ada",
    )(c, w, b)


def _shifted(full, k):
    return pltpu.roll(full, k, axis=0)[HALO:]


def _mixer_kernel(x_ref, mod_ref, w_main_ref, w_ba_ref, conv_w_ref, conv_nw_ref, dn_conv_w_ref,
                  a_row_ref, dt_row_ref, dn_nw_ref, w_out_ref, ln_g_ref, ln_b_ref, w_rt_ref, b_rt_ref,
                  g64_ref, g128_ref, tri_ref, e_beta_ref, e_g_ref, lvl_ref,
                  x1_ref, h2_ref, logit_ref,
                  cu_buf, qkv_buf, s_ref, q_s, k_s, v_s, o_s, beta_s, gc_s, u_s,
                  w_s, qkg_s, qd_s, kend_t_s):
    ts = x_ref.shape[1]

    @pl.when(pl.program_id(1) == 0)
    def _():
        cu_buf[0:HALO, :] = jnp.zeros((HALO, D_CONV), F32)
        qkv_buf[0:HALO, :] = jnp.zeros((HALO, 3 * D_DELTA), F32)
        s_ref[...] = jnp.zeros(s_ref.shape, F32)

    x = x_ref[0]
    shift1 = mod_ref[0, 0:1, :]
    scale1 = mod_ref[0, 1:2, :]
    gate1 = mod_ref[0, 2:3, :]
    shift2 = mod_ref[0, 3:4, :]
    scale2 = mod_ref[0, 4:5, :]
    h = (x * (1.0 + scale1) + shift1).astype(BF16)

    b_gate = _dot(h, w_main_ref[:, 0:D_CONV])
    c_gate = _dot(h, w_main_ref[:, D_CONV:2 * D_CONV])
    u_in = _dot(h, w_main_ref[:, 2 * D_CONV:3 * D_CONV])
    cu = c_gate * u_in
    cu_buf[HALO:HALO + ts, :] = cu
    full = cu_buf[...]
    cw = conv_w_ref[...]
    conv = cw[0:1, :] * _shifted(full, 2) + cw[1:2, :] * _shifted(full, 1) + cw[2:3, :] * cu
    cu_buf[0:HALO, :] = cu[ts - HALO:ts, :]
    y = b_gate * conv
    ms = _group_sums(y * y, g64_ref[...]) * (1.0 / CONV_GROUP)
    y_conv = y * lax.rsqrt(ms + RMS_EPS) * conv_nw_ref[...]

    off = 3 * D_CONV
    qkv = _dot(h, w_main_ref[:, off:off + 3 * D_DELTA])
    qkv_buf[HALO:HALO + ts, :] = qkv
    fullq = qkv_buf[...]
    w4 = dn_conv_w_ref[...]
    acc = (w4[0:1, :] * _shifted(fullq, 3) + w4[1:2, :] * _shifted(fullq, 2)
           + w4[2:3, :] * _shifted(fullq, 1) + w4[3:4, :] * qkv)
    qkv_buf[0:HALO, :] = qkv[ts - HALO:ts, :]
    act = _silu(acc)
    q = act[:, 0:D_DELTA]
    k = act[:, D_DELTA:2 * D_DELTA]
    g128 = g128_ref[...]
    q_s[...] = q * lax.rsqrt(_group_sums(q * q, g128) + RMS_EPS) * (DN_HEAD_DIM ** -0.5)
    k_s[...] = k * lax.rsqrt(_group_sums(k * k, g128) + RMS_EPS)
    v_s[...] = act[:, 2 * D_DELTA:3 * D_DELTA]

    ba = _dot(h, w_ba_ref[...])
    beta_all = _sigmoid(ba)
    a_in = ba + dt_row_ref[...]
    softplus = jnp.maximum(a_in, 0.0) + jnp.log1p(jnp.exp(-jnp.abs(a_in)))
    g_all = -jnp.exp(a_row_ref[...]) * softplus
    gc_all = _dot_split_rhs(tri_ref[...], g_all)
    beta_s[...] = _dot_split_lhs(beta_all, e_beta_ref[...])
    gc_s[...] = _dot_split_lhs(gc_all, e_g_ref[...])

    row = lax.broadcasted_iota(jnp.int32, (CHUNK, CHUNK), 0)
    col = lax.broadcasted_iota(jnp.int32, (CHUNK, CHUNK), 1)
    incl = row >= col
    eye = (row == col).astype(F32)
    blocks = [(slice(c * CHUNK, (c + 1) * CHUNK), slice(hd * DN_HEAD_DIM, (hd + 1) * DN_HEAD_DIM))
              for c in range(ts // CHUNK) for hd in range(DN_HEADS)]

    lmats, t_invs = [], []
    for rows, cols in blocks:
        qh = q_s[rows, cols]
        kh = k_s[rows, cols]
        gc = gc_s[rows, cols]
        gam = jnp.exp(jnp.where(incl, gc - gc.T, -jnp.inf))
        kb = kh.astype(BF16)
        lmat = beta_s[rows, cols] * _dot_nt(kb, kb) * gam
        qkg_s[rows, cols] = (_dot_nt(qh.astype(BF16), kb) * gam).astype(BF16)
        eg = jnp.exp(gc)
        qd_s[rows, cols] = (qh * eg).astype(BF16)
        k_end = kh * jnp.exp(gc[CHUNK - 1:CHUNK, :] - gc)
        kend_t_s[rows, cols] = k_end.T.astype(BF16)
        lmats.append(lmat)
        t_invs.append(eye - lmat * lvl_ref[0])
    for lv in range(1, N_LEVELS):
        tbs = [t.astype(BF16) for t in t_invs]
        ys = [_dot((lm * lvl_ref[lv]).astype(BF16), tb).astype(BF16) for lm, tb in zip(lmats, tbs)]
        t_invs = [t - _dot(tb, y) for t, tb, y in zip(t_invs, tbs, ys)]
    for (rows, cols), t_inv in zip(blocks, t_invs):
        beta = beta_s[rows, cols]
        kh = k_s[rows, cols]
        rhs = jnp.concatenate([beta * v_s[rows, cols], beta * jnp.exp(gc_s[rows, cols]) * kh], axis=1)
        sol = _dot(t_inv.astype(BF16), rhs.astype(BF16))
        u_s[rows, cols] = sol[:, 0:DN_HEAD_DIM]
        w_s[rows, cols] = sol[:, DN_HEAD_DIM:2 * DN_HEAD_DIM].astype(BF16)

    heads = [slice(hd * DN_HEAD_DIM, (hd + 1) * DN_HEAD_DIM) for hd in range(DN_HEADS)]
    for c in range(ts // CHUNK):
        rows = slice(c * CHUNK, (c + 1) * CHUNK)
        states = [s_ref[hd] for hd in range(DN_HEADS)]
        sbs = [st.astype(BF16) for st in states]
        dbs = [(u_s[rows, cols] - _dot(w_s[rows, cols], sb)).astype(BF16) for cols, sb in zip(heads, sbs)]
        for hd, cols in enumerate(heads):
            o_s[rows, cols] = _dot(qd_s[rows, cols], sbs[hd]) + _dot(qkg_s[rows, cols], dbs[hd])
            decay = jnp.exp(gc_s[(c + 1) * CHUNK - 1:(c + 1) * CHUNK, cols])
            s_ref[hd] = states[hd] * decay + _dot(kend_t_s[rows, cols], dbs[hd])

    o = o_s[...]
    z = _dot(h, w_main_ref[:, off + 3 * D_DELTA:off + 4 * D_DELTA])
    o_ms = _group_sums(o * o, g128) * (1.0 / DN_HEAD_DIM)
    y_dn = o * lax.rsqrt(o_ms + RMS_EPS) * dn_nw_ref[...] * _silu(z)
    mix_in = jnp.concatenate([y_conv, y_dn], axis=1).astype(BF16)
    mix = _dot(mix_in, w_out_ref[...])
    r = DEEP_ALPHA * x + gate1 * mix
    mu = jnp.mean(r, axis=-1, keepdims=True)
    rc = r - mu
    var = jnp.mean(rc * rc, axis=-1, keepdims=True)
    x1 = rc * lax.rsqrt(var + LN_EPS) * ln_g_ref[...] + ln_b_ref[...]
    x1_ref[0] = x1
    h2 = x1 * (1.0 + scale2) + shift2
    _store_rows(h2_ref.at[0], h2)
    logit_ref[0] = _dot(h2.astype(BF16), w_rt_ref[...]) + b_rt_ref[...]


def _const_spec(shape):
    nd = len(shape)
    return pl.BlockSpec(shape, lambda b, j, _nd=nd: (0,) * _nd, pipeline_mode=pl.Buffered(1))


def _mixer_constants(ts):
    m = np.arange(MXU_TILE)
    g64 = (m[:, None] // CONV_GROUP == m[None, :] // CONV_GROUP).astype(np.float32)
    g128 = (m[:, None] // DN_HEAD_DIM == m[None, :] // DN_HEAD_DIM).astype(np.float32)
    i = np.arange(D_DELTA)
    t = np.arange(ts)
    tri = ((t[:, None] // CHUNK == t[None, :] // CHUNK) & (t[None, :] <= t[:, None])).astype(np.float32)
    lane = np.arange(LANES)
    e_beta = (lane[:, None] == i[None, :] // DN_HEAD_DIM).astype(np.float32)
    e_g = (lane[:, None] == DN_HEADS + i[None, :] // DN_HEAD_DIM).astype(np.float32)
    r = np.arange(CHUNK)
    lvl = np.stack([(((r[:, None] >> l) ^ (r[None, :] >> l)) == 1) & (r[:, None] > r[None, :])
                    for l in range(N_LEVELS)]).astype(np.float32)
    as_bf16 = lambda a: jnp.asarray(a, dtype=BF16)
    return as_bf16(g64), as_bf16(g128), as_bf16(tri), as_bf16(e_beta), as_bf16(e_g), jnp.asarray(lvl)


def _mixer(x, mod, w_in, conv_w, conv_norm_w, dn_conv_w, dn_a_log, dn_dt_bias, dn_norm_w, w_out,
           ln_g, ln_b, w_grp, b_grp, w_exp, b_exp):
    n_b, seq, d = x.shape
    ts = SEQ_TILE
    n_main = 3 * D_CONV + 4 * D_DELTA
    w_main = w_in[:, :n_main].astype(BF16)
    w_ba = jnp.pad(w_in[:, n_main:], ((0, 0), (0, LANES - 2 * DN_HEADS))).astype(BF16)
    head_pad = (DN_HEADS, LANES - 2 * DN_HEADS)
    a_row = jnp.pad(dn_a_log, head_pad).reshape(1, LANES)
    dt_row = jnp.pad(dn_dt_bias, head_pad).reshape(1, LANES)
    n_rt = N_GROUPS + N_EXPERTS
    w_rt = jnp.pad(jnp.concatenate([w_grp, w_exp], axis=1), ((0, 0), (0, LANES - n_rt))).astype(BF16)
    b_rt = jnp.pad(jnp.concatenate([b_grp, b_exp]), (0, LANES - n_rt)).reshape(1, LANES)
    consts = _mixer_constants(ts)
    operands = (
        x, mod, w_main, w_ba, conv_w, conv_norm_w.reshape(1, D_CONV), dn_conv_w, a_row, dt_row,
        jnp.tile(dn_norm_w, DN_HEADS).reshape(1, D_DELTA), w_out.astype(BF16),
        ln_g.reshape(1, d), ln_b.reshape(1, d), w_rt, b_rt) + consts
    in_specs = [
        pl.BlockSpec((1, ts, d), lambda b, j: (b, j, 0)),
        pl.BlockSpec((1,) + mod.shape[1:], lambda b, j: (b, 0, 0)),
    ] + [_const_spec(a.shape) for a in operands[2:]]
    tile_spec = lambda width: pl.BlockSpec((1, ts, width), lambda b, j: (b, j, 0))
    return pl.pallas_call(
        _mixer_kernel,
        grid=(n_b, seq // ts),
        in_specs=in_specs,
        out_specs=[tile_spec(d), pl.BlockSpec((1, ts * ROW_LINES, LANES), lambda b, j: (b, j, 0)),
                   tile_spec(LANES)],
        out_shape=[jax.ShapeDtypeStruct((n_b, seq, d), F32),
                   jax.ShapeDtypeStruct((n_b, seq * ROW_LINES, LANES), F32),
                   jax.ShapeDtypeStruct((n_b, seq, LANES), F32)],
        scratch_shapes=[
            pltpu.VMEM((HALO + ts, D_CONV), F32),
            pltpu.VMEM((HALO + ts, 3 * D_DELTA), F32),
            pltpu.VMEM((DN_HEADS, DN_HEAD_DIM, DN_HEAD_DIM), F32),
        ] + [pltpu.VMEM((ts, D_DELTA), F32)] * 7 + [pltpu.VMEM((ts, D_DELTA), BF16)] * 4,
        compiler_params=pltpu.CompilerParams(dimension_semantics=("arbitrary", "arbitrary"),
                                             vmem_limit_bytes=VMEM_LIMIT),
        name="mixer",
    )(*operands)


def _route_kernel(lg_ref, o_ref, cnt_ref):
    lg = lg_ref[...]
    lane = lax.broadcasted_iota(jnp.int32, lg.shape, 1).astype(F32)
    neg = -jnp.inf
    big = float(LANES)
    is_grp = lane < N_GROUPS
    gl = jnp.where(is_grp, lg, neg)
    gm = jnp.max(gl, axis=-1, keepdims=True)
    gi = jnp.min(jnp.where(gl == gm, lane, big), axis=-1, keepdims=True)
    grp_w = 1.0 / jnp.sum(jnp.where(is_grp, jnp.exp(lg - gm), 0.0), axis=-1, keepdims=True)
    lo = N_GROUPS + gi * EXPERTS_PER_GROUP
    el = jnp.where((lane >= lo) & (lane < lo + EXPERTS_PER_GROUP), lg, neg)
    m1 = jnp.max(el, axis=-1, keepdims=True)
    i1 = jnp.min(jnp.where(el == m1, lane, big), axis=-1, keepdims=True)
    el2 = jnp.where(lane == i1, neg, el)
    m2 = jnp.max(el2, axis=-1, keepdims=True)
    i2 = jnp.min(jnp.where(el2 == m2, lane, big), axis=-1, keepdims=True)
    ratio = jnp.exp(m2 - m1)
    g0 = grp_w / (1.0 + ratio)
    g1 = g0 * ratio
    e0 = i1 - N_GROUPS
    e1 = i2 - N_GROUPS
    o_ref[...] = jnp.where(lane == 0, e0, jnp.where(lane == 1, e1, jnp.where(lane == 2, g0,
                                                                            jnp.where(lane == 3, g1, 0.0))))
    chosen = ((lane == e0) | (lane == e1)).astype(F32)
    cnt_ref[...] = jnp.broadcast_to(jnp.sum(chosen, axis=0, keepdims=True), cnt_ref.shape)


def _route(logits):
    n_tok = logits.shape[0]
    n_tiles = n_tok // ROUTE_TILE
    spec = pl.BlockSpec((ROUTE_TILE, LANES), lambda i: (i, 0))
    route, cnt = pl.pallas_call(
        _route_kernel,
        grid=(n_tiles,),
        in_specs=[spec],
        out_specs=[spec, pl.BlockSpec((SUBLANES, LANES), lambda i: (i, 0))],
        out_shape=[jax.ShapeDtypeStruct((n_tok, LANES), F32),
                   jax.ShapeDtypeStruct((n_tiles * SUBLANES, LANES), F32)],
        name="route",
    )(logits)
    counts = jnp.sum(cnt.reshape(n_tiles, SUBLANES, LANES)[:, 0, :N_EXPERTS], axis=0).astype(jnp.int32)
    return route, counts


def _block_table(counts, n_asg):
    padded = ((counts + ROW_BLOCK - 1) // ROW_BLOCK) * ROW_BLOCK
    pend = jnp.cumsum(padded)
    pstart = pend - padded
    n_blocks = n_asg // ROW_BLOCK + N_EXPERTS
    block_start = jnp.arange(n_blocks, dtype=jnp.int32) * ROW_BLOCK
    block_expert = jnp.minimum(jnp.sum(pend[None, :] <= block_start[:, None], axis=1),
                               N_EXPERTS - 1).astype(jnp.int32)
    n_valid = jnp.clip((pstart + counts)[block_expert] - block_start, 0, ROW_BLOCK).astype(jnp.int32)
    n_used = (pend[-1:] // ROW_BLOCK).astype(jnp.int32)
    return pstart, block_expert, n_valid, n_used


def _pos_kernel(route_ref, pstart_ref, tri_ref, pos_ref, running):
    @pl.when(pl.program_id(0) == 0)
    def _():
        running[...] = jnp.zeros(running.shape, F32)

    rt = route_ref[...]
    lane = lax.broadcasted_iota(jnp.int32, rt.shape, 1).astype(F32)
    oh0 = (lane == rt[:, 0:1]).astype(F32)
    oh1 = (lane == rt[:, 1:2]).astype(F32)
    both = oh0 + oh1
    before = pstart_ref[...] + running[0:1, :] + _dot(tri_ref[...], both.astype(BF16))
    pos0 = jnp.sum(oh0 * before, axis=-1, keepdims=True)
    pos1 = jnp.sum(oh1 * before, axis=-1, keepdims=True)
    pos_ref[...] = jnp.where(lane == 0, pos0, jnp.where(lane == 1, pos1, 0.0)).astype(jnp.int32)
    running[...] = running[...] + jnp.sum(both, axis=0, keepdims=True)


def _positions(route, pstart):
    n_tok = route.shape[0]
    t = np.arange(ROUTE_TILE)
    tri = jnp.asarray((t[None, :] < t[:, None]).astype(np.float32), dtype=BF16)
    pstart_row = jnp.pad(pstart.astype(F32), (0, LANES - N_EXPERTS)).reshape(1, LANES)
    spec = pl.BlockSpec((ROUTE_TILE, LANES), lambda i: (i, 0))
    return pl.pallas_call(
        _pos_kernel,
        grid=(n_tok // ROUTE_TILE,),
        in_specs=[spec, pl.BlockSpec((1, LANES), lambda i: (0, 0)),
                  pl.BlockSpec((ROUTE_TILE, ROUTE_TILE), lambda i: (0, 0), pipeline_mode=pl.Buffered(1))],
        out_specs=spec,
        out_shape=jax.ShapeDtypeStruct((n_tok, LANES), jnp.int32),
        scratch_shapes=[pltpu.VMEM((SUBLANES, LANES), F32)],
        compiler_params=pltpu.CompilerParams(dimension_semantics=("arbitrary",)),
        name="positions",
    )(route, pstart_row, tri)


def _dispatch_kernel(nv_ref, pos_ref, h2_hbm, x_hbm, zbuf, sem_fill, sem_rows):
    i = pl.program_id(0)
    n_blocks = nv_ref.shape[0]
    n_tok_step = pos_ref.shape[2] // TOP_K
    step_lines = pos_ref.shape[2] * ROW_LINES

    def fill(b):
        return pltpu.make_async_copy(zbuf, x_hbm.at[_lines(b * ROW_BLOCK, ROW_BLOCK)], sem_fill)

    def wait_rows():
        pltpu.make_async_copy(h2_hbm.at[pl.ds(0, step_lines)], x_hbm.at[pl.ds(0, step_lines)], sem_rows).wait()

    @pl.when(i == 0)
    def _():
        zbuf[...] = jnp.zeros(zbuf.shape, F32)

        def start_fill(b, carry):
            @pl.when(nv_ref[b] < ROW_BLOCK)
            def _():
                fill(b).start()
            return carry

        def wait_fill(b, carry):
            @pl.when(nv_ref[b] < ROW_BLOCK)
            def _():
                fill(b).wait()
            return carry

        lax.fori_loop(0, n_blocks, start_fill, 0)
        lax.fori_loop(0, n_blocks, wait_fill, 0)

    tok0 = i * n_tok_step

    def group(g, carry):
        for j in range(DMA_UNROLL):
            t = g * DMA_UNROLL + j
            src = h2_hbm.at[_lines(tok0 + t)]
            for k in range(TOP_K):
                pltpu.make_async_copy(src, x_hbm.at[_lines(pos_ref[0, 0, t * TOP_K + k])],
                                      sem_rows).start(priority=k)
        return carry

    lax.fori_loop(0, n_tok_step // DMA_UNROLL, group, 0)

    @pl.when(i > 0)
    def _():
        wait_rows()

    @pl.when(i == pl.num_programs(0) - 1)
    def _():
        wait_rows()


def _dispatch(h2_lines, pos_steps, n_valid):
    n_blocks = n_valid.shape[0]
    grid_spec = pltpu.PrefetchScalarGridSpec(
        num_scalar_prefetch=1,
        grid=(pos_steps.shape[0],),
        in_specs=[pl.BlockSpec((1, 1, pos_steps.shape[2]), lambda i, nv: (i, 0, 0), memory_space=pltpu.SMEM),
                  pl.BlockSpec(memory_space=pl.ANY)],
        out_specs=pl.BlockSpec(memory_space=pl.ANY),
        scratch_shapes=[pltpu.VMEM((ROW_BLOCK * ROW_LINES, LANES), F32),
                        pltpu.SemaphoreType.DMA(()), pltpu.SemaphoreType.DMA(())],
    )
    return pl.pallas_call(
        _dispatch_kernel,
        grid_spec=grid_spec,
        out_shape=jax.ShapeDtypeStruct((n_blocks * ROW_BLOCK * ROW_LINES, LANES), F32),
        compiler_params=pltpu.CompilerParams(dimension_semantics=("arbitrary",)),
        name="dispatch",
    )(n_valid, pos_steps, h2_lines)


def _expert_kernel(be_ref, nu_ref, x_ref, wg_ref, wu_ref, wd_ref, y_ref):
    del be_ref
    used = pl.program_id(0) < nu_ref[0]

    @pl.when(used)
    def _():
        xb = _load_rows(x_ref, ROW_BLOCK).astype(BF16)
        gate = _dot(xb, wg_ref[0])
        up = _dot(xb, wu_ref[0])
        _store_rows(y_ref, _dot((_silu(gate) * up).astype(BF16), wd_ref[0]))

    @pl.when(jnp.logical_not(used))
    def _():
        y_ref[...] = jnp.zeros(y_ref.shape, F32)


def _experts(x_lines, block_expert, n_used, w_gate, w_up, w_down):
    n_blocks = block_expert.shape[0]
    d = w_gate.shape[1]
    block_lines = ROW_BLOCK * ROW_LINES
    last_used = lambda i, nu: jnp.minimum(i, jnp.maximum(nu[0] - 1, 0))
    w_spec = lambda shape: pl.BlockSpec(shape, lambda i, be, nu: (be[last_used(i, nu)], 0, 0))
    grid_spec = pltpu.PrefetchScalarGridSpec(
        num_scalar_prefetch=2,
        grid=(n_blocks,),
        in_specs=[
            pl.BlockSpec((block_lines, LANES), lambda i, be, nu: (last_used(i, nu), 0)),
            w_spec((1, d, D_EXPERT)), w_spec((1, d, D_EXPERT)), w_spec((1, D_EXPERT, d)),
        ],
        out_specs=pl.BlockSpec((block_lines, LANES), lambda i, be, nu: (i, 0)),
    )
    return pl.pallas_call(
        _expert_kernel,
        grid_spec=grid_spec,
        out_shape=jax.ShapeDtypeStruct(x_lines.shape, F32),
        compiler_params=pltpu.CompilerParams(dimension_semantics=("arbitrary",)),
        name="experts",
    )(block_expert, n_used, x_lines, w_gate.astype(BF16), w_up.astype(BF16), w_down.astype(BF16))


def _final_kernel(pos_ref, pos_next_ref, x1_ref, mod_ref, route_ref, ln_g_ref, ln_b_ref, y_hbm, o_ref,
                  ybuf, sem):
    i = pl.program_id(0)
    slot = i % 2
    tt = x1_ref.shape[1]

    def start_gather(idx_ref, to_slot):
        def group(g, carry):
            for j in range(DMA_UNROLL):
                t = g * DMA_UNROLL + j
                for k in range(TOP_K):
                    pltpu.make_async_copy(y_hbm.at[_lines(idx_ref[0, 0, t * TOP_K + k])],
                                          ybuf.at[to_slot, _lines(k * tt + t)],
                                          sem.at[to_slot]).start(priority=k)
            return carry
        lax.fori_loop(0, tt // DMA_UNROLL, group, 0)

    @pl.when(i == 0)
    def _():
        start_gather(pos_ref, 0)

    @pl.when(i < pl.num_programs(0) - 1)
    def _():
        start_gather(pos_next_ref, 1 - slot)

    pltpu.make_async_copy(y_hbm.at[pl.ds(0, ybuf.shape[1])], ybuf.at[slot], sem.at[slot]).wait()
    x1 = x1_ref[0]
    gate2 = mod_ref[0, 5:6, :]
    rt = route_ref[...]
    rows = ybuf.at[slot]
    y0 = _load_rows(rows.at[pl.ds(0, tt * ROW_LINES)], tt)
    y1 = _load_rows(rows.at[pl.ds(tt * ROW_LINES, tt * ROW_LINES)], tt)
    r = DEEP_ALPHA * x1 + gate2 * (rt[:, 2:3] * y0 + rt[:, 3:4] * y1)
    mu = jnp.mean(r, axis=-1, keepdims=True)
    rc = r - mu
    var = jnp.mean(rc * rc, axis=-1, keepdims=True)
    o_ref[0] = rc * lax.rsqrt(var + LN_EPS) * ln_g_ref[...] + ln_b_ref[...]


def _final(x1, mod, y_lines, route, pos_steps, ln_g, ln_b):
    n_b, seq, d = x1.shape
    tt = FINAL_TILE
    per_b = seq // tt
    n_steps = n_b * per_b
    pos_spec = lambda shift: pl.BlockSpec((1, 1, TOP_K * tt), lambda i: (jnp.minimum(i + shift, n_steps - 1), 0, 0),
                                          memory_space=pltpu.SMEM)
    return pl.pallas_call(
        _final_kernel,
        grid=(n_steps,),
        in_specs=[
            pos_spec(0), pos_spec(1),
            pl.BlockSpec((1, tt, d), lambda i: (i // per_b, i % per_b, 0)),
            pl.BlockSpec((1,) + mod.shape[1:], lambda i: (i // per_b, 0, 0)),
            pl.BlockSpec((tt, LANES), lambda i: (i, 0)),
            pl.BlockSpec((1, d), lambda i: (0, 0)),
            pl.BlockSpec((1, d), lambda i: (0, 0)),
            pl.BlockSpec(memory_space=pl.ANY),
        ],
        out_specs=pl.BlockSpec((1, tt, d), lambda i: (i // per_b, i % per_b, 0)),
        out_shape=jax.ShapeDtypeStruct((n_b, seq, d), F32),
        scratch_shapes=[pltpu.VMEM((2, TOP_K * tt * ROW_LINES, LANES), F32), pltpu.SemaphoreType.DMA((2,))],
        compiler_params=pltpu.CompilerParams(dimension_semantics=("arbitrary",), vmem_limit_bytes=VMEM_LIMIT),
        name="final",
    )(pos_steps, pos_steps, x1, mod, route, ln_g.reshape(1, d), ln_b.reshape(1, d), y_lines)


def kernel(x, c, w_ada, b_ada, w_in, conv_w, conv_norm_w, dn_conv_w, dn_A_log, dn_dt_bias, dn_norm_w,
           w_out, ln1_g, ln1_b, w_grp, b_grp, w_exp, b_exp, w_gate, w_up, w_down, ln2_g, ln2_b):
    n_b, seq, d = x.shape
    n_tok = n_b * seq
    for l in range(w_ada.shape[0]):
        mod = _ada(c, w_ada[l], b_ada[l].reshape(1, -1)).reshape(n_b, 6, d)
        x1, h2, logits = _mixer(x, mod, w_in[l], conv_w[l], conv_norm_w[l], dn_conv_w[l], dn_A_log[l],
                                dn_dt_bias[l], dn_norm_w[l], w_out[l], ln1_g[l], ln1_b[l],
                                w_grp[l], b_grp[l], w_exp[l], b_exp[l])
        route, counts = _route(logits.reshape(n_tok, LANES))
        pstart, block_expert, n_valid, n_used = _block_table(counts, n_tok * TOP_K)
        pos = _positions(route, pstart)[:, 0:TOP_K]
        pos_steps = pos.reshape(n_tok // FINAL_TILE, 1, TOP_K * FINAL_TILE)
        x_rows = _dispatch(h2.reshape(n_tok * ROW_LINES, LANES), pos_steps, n_valid)
        y_rows = _experts(x_rows, block_expert, n_used, w_gate[l], w_up[l], w_down[l])
        x = _final(x1, mod, y_rows, route, pos_steps, ln2_g[l], ln2_b[l])
    return x
```

```python
import functools

import numpy as np
import jax
import jax.numpy as jnp
from jax import lax
from jax.experimental import pallas as pl
from jax.experimental.pallas import tpu as pltpu

F32 = jnp.float32
BF16 = jnp.bfloat16

D_MODEL = 1024
D_CONV = 512
CONV_GROUP = 64
DN_HEADS = 4
DN_HEAD_DIM = 128
D_DELTA = DN_HEADS * DN_HEAD_DIM
N_GROUPS = 4
EXPERTS_PER_GROUP = 8
N_EXPERTS = N_GROUPS * EXPERTS_PER_GROUP
TOP_K = 2
D_EXPERT = 256
LN_EPS = 1e-5
RMS_EPS = 1e-6
DEEP_ALPHA = 2.0 ** 0.25

LANES = 128
MXU_TILE = 256
SUBLANES = 8
HALO = SUBLANES
SEQ_TILE = 512
CHUNK = 128
N_LEVELS = 7
ROUTE_TILE = 1024
ROW_BLOCK = 256
ROW_LINES = D_MODEL // LANES
DMA_UNROLL = 8
FINAL_TILE = 512
VMEM_LIMIT = 56 * 1024 * 1024


def _dot(a, b):
    return jnp.dot(a, b, preferred_element_type=F32)


def _dot_nt(a, b):
    return lax.dot_general(a, b, (((1,), (1,)), ((), ())), preferred_element_type=F32)


def _split(x):
    hi = x.astype(BF16)
    lo = (x - hi.astype(F32)).astype(BF16)
    return hi, lo


def _dot_split_lhs(x, m):
    hi, lo = _split(x)
    return _dot(hi, m) + _dot(lo, m)


def _group_sums(x, g):
    w = g.shape[0]
    return jnp.concatenate([_dot_split_lhs(x[:, i:i + w], g) for i in range(0, x.shape[1], w)], axis=1)


def _dot_split_rhs(m, x):
    hi, lo = _split(x)
    return _dot(m, hi) + _dot(m, lo)


def _sigmoid(x):
    return 1.0 / (1.0 + jnp.exp(-x))


def _silu(x):
    return x * _sigmoid(x)


def _load_rows(ref, n_rows):
    return jnp.concatenate([ref[pl.ds(s, n_rows, stride=ROW_LINES), :] for s in range(ROW_LINES)], axis=1)


def _store_rows(ref, val):
    for s in range(ROW_LINES):
        ref[pl.ds(s, val.shape[0], stride=ROW_LINES), :] = val[:, s * LANES:(s + 1) * LANES]


def _lines(row, n=1):
    return pl.ds(pl.multiple_of(row * ROW_LINES, ROW_LINES), n * ROW_LINES)


def _ada_kernel(c_ref, w_ref, b_ref, o_ref):
    c = c_ref[...]
    o_ref[...] = _dot(_silu(c).astype(BF16), w_ref[...].astype(BF16)) + b_ref[...]


def _ada(c, w, b):
    n_b, d = c.shape
    n_out = w.shape[1]
    return pl.pallas_call(
        _ada_kernel,
        grid=(n_out // d,),
        in_specs=[
            pl.BlockSpec((n_b, d), lambda i: (0, 0)),
            pl.BlockSpec((d, d), lambda i: (0, i)),
            pl.BlockSpec((1, d), lambda i: (0, i)),
        ],
        out_specs=pl.BlockSpec((n_b, d), lambda i: (0, i)),
        out_shape=jax.ShapeDtypeStruct((n_b, n_out), F32),
        name="ada",
    )(c, w, b)


def _shifted(full, k):
    return pltpu.roll(full, k, axis=0)[HALO:]


def _mixer_kernel(x_ref, mod_ref, w_main_ref, w_ba_ref, conv_w_ref, conv_nw_ref, dn_conv_w_ref,
                  a_row_ref, dt_row_ref, dn_nw_ref, w_out_ref, ln_g_ref, ln_b_ref, w_rt_ref, b_rt_ref,
                  g64_ref, g128_ref, tri_ref, e_beta_ref, e_g_ref, lvl_ref,
                  x1_ref, h2_ref, logit_ref,
                  cu_buf, qkv_buf, s_ref, q_s, k_s, v_s, o_s, beta_s, gc_s, u_s,
                  w_s, qkg_s, qd_s, kend_t_s):
    ts = x_ref.shape[1]

    @pl.when(pl.program_id(1) == 0)
    def _():
        cu_buf[0:HALO, :] = jnp.zeros((HALO, D_CONV), F32)
        qkv_buf[0:HALO, :] = jnp.zeros((HALO, 3 * D_DELTA), F32)
        s_ref[...] = jnp.zeros(s_ref.shape, F32)

    x = x_ref[0]
    shift1 = mod_ref[0, 0:1, :]
    scale1 = mod_ref[0, 1:2, :]
    gate1 = mod_ref[0, 2:3, :]
    shift2 = mod_ref[0, 3:4, :]
    scale2 = mod_ref[0, 4:5, :]
    h = (x * (1.0 + scale1) + shift1).astype(BF16)

    b_gate = _dot(h, w_main_ref[:, 0:D_CONV])
    c_gate = _dot(h, w_main_ref[:, D_CONV:2 * D_CONV])
    u_in = _dot(h, w_main_ref[:, 2 * D_CONV:3 * D_CONV])
    cu = c_gate * u_in
    cu_buf[HALO:HALO + ts, :] = cu
    full = cu_buf[...]
    cw = conv_w_ref[...]
    conv = cw[0:1, :] * _shifted(full, 2) + cw[1:2, :] * _shifted(full, 1) + cw[2:3, :] * cu
    cu_buf[0:HALO, :] = cu[ts - HALO:ts, :]
    y = b_gate * conv
    ms = _group_sums(y * y, g64_ref[...]) * (1.0 / CONV_GROUP)
    y_conv = y * lax.rsqrt(ms + RMS_EPS) * conv_nw_ref[...]

    off = 3 * D_CONV
    qkv = _dot(h, w_main_ref[:, off:off + 3 * D_DELTA])
    qkv_buf[HALO:HALO + ts, :] = qkv
    fullq = qkv_buf[...]
    w4 = dn_conv_w_ref[...]
    acc = (w4[0:1, :] * _shifted(fullq, 3) + w4[1:2, :] * _shifted(fullq, 2)
           + w4[2:3, :] * _shifted(fullq, 1) + w4[3:4, :] * qkv)
    qkv_buf[0:HALO, :] = qkv[ts - HALO:ts, :]
    act = _silu(acc)
    q = act[:, 0:D_DELTA]
    k = act[:, D_DELTA:2 * D_DELTA]
    g128 = g128_ref[...]
    q_s[...] = q * lax.rsqrt(_group_sums(q * q, g128) + RMS_EPS) * (DN_HEAD_DIM ** -0.5)
    k_s[...] = k * lax.rsqrt(_group_sums(k * k, g128) + RMS_EPS)
    v_s[...] = act[:, 2 * D_DELTA:3 * D_DELTA]

    ba = _dot(h, w_ba_ref[...])
    beta_all = _sigmoid(ba)
    a_in = ba + dt_row_ref[...]
    softplus = jnp.maximum(a_in, 0.0) + jnp.log1p(jnp.exp(-jnp.abs(a_in)))
    g_all = -jnp.exp(a_row_ref[...]) * softplus
    gc_all = _dot_split_rhs(tri_ref[...], g_all)
    beta_s[...] = _dot_split_lhs(beta_all, e_beta_ref[...])
    gc_s[...] = _dot_split_lhs(gc_all, e_g_ref[...])

    row = lax.broadcasted_iota(jnp.int32, (CHUNK, CHUNK), 0)
    col = lax.broadcasted_iota(jnp.int32, (CHUNK, CHUNK), 1)
    incl = row >= col
    eye = (row == col).astype(F32)
    blocks = [(slice(c * CHUNK, (c + 1) * CHUNK), slice(hd * DN_HEAD_DIM, (hd + 1) * DN_HEAD_DIM))
              for c in range(ts // CHUNK) for hd in range(DN_HEADS)]

    lmats, t_invs = [], []
    for rows, cols in blocks:
        qh = q_s[rows, cols]
        kh = k_s[rows, cols]
        gc = gc_s[rows, cols]
        gam = jnp.exp(jnp.where(incl, gc - gc.T, -jnp.inf))
        kb = kh.astype(BF16)
        lmat = beta_s[rows, cols] * _dot_nt(kb, kb) * gam
        qkg_s[rows, cols] = (_dot_nt(qh.astype(BF16), kb) * gam).astype(BF16)
        eg = jnp.exp(gc)
        qd_s[rows, cols] = (qh * eg).astype(BF16)
        k_end = kh * jnp.exp(gc[CHUNK - 1:CHUNK, :] - gc)
        kend_t_s[rows, cols] = k_end.T.astype(BF16)
        lmats.append(lmat)
        t_invs.append(eye - lmat * lvl_ref[0])
    for lv in range(1, N_LEVELS):
        tbs = [t.astype(BF16) for t in t_invs]
        ys = [_dot((lm * lvl_ref[lv]).astype(BF16), tb).astype(BF16) for lm, tb in zip(lmats, tbs)]
        t_invs = [t - _dot(tb, y) for t, tb, y in zip(t_invs, tbs, ys)]
    for (rows, cols), t_inv in zip(blocks, t_invs):
        beta = beta_s[rows, cols]
        kh = k_s[rows, cols]
        rhs = jnp.concatenate([beta * v_s[rows, cols], beta * jnp.exp(gc_s[rows, cols]) * kh], axis=1)
        sol = _dot(t_inv.astype(BF16), rhs.astype(BF16))
        u_s[rows, cols] = sol[:, 0:DN_HEAD_DIM]
        w_s[rows, cols] = sol[:, DN_HEAD_DIM:2 * DN_HEAD_DIM].astype(BF16)

    heads = [slice(hd * DN_HEAD_DIM, (hd + 1) * DN_HEAD_DIM) for hd in range(DN_HEADS)]
    for c in range(ts // CHUNK):
        rows = slice(c * CHUNK, (c + 1) * CHUNK)
        states = [s_ref[hd] for hd in range(DN_HEADS)]
        sbs = [st.astype(BF16) for st in states]
        dbs = [(u_s[rows, cols] - _dot(w_s[rows, cols], sb)).astype(BF16) for cols, sb in zip(heads, sbs)]
        for hd, cols in enumerate(heads):
            o_s[rows, cols] = _dot(qd_s[rows, cols], sbs[hd]) + _dot(qkg_s[rows, cols], dbs[hd])
            decay = jnp.exp(gc_s[(c + 1) * CHUNK - 1:(c + 1) * CHUNK, cols])
            s_ref[hd] = states[hd] * decay + _dot(kend_t_s[rows, cols], dbs[hd])

    o = o_s[...]
    z = _dot(h, w_main_ref[:, off + 3 * D_DELTA:off + 4 * D_DELTA])
    o_ms = _group_sums(o * o, g128) * (1.0 / DN_HEAD_DIM)
    y_dn = o * lax.rsqrt(o_ms + RMS_EPS) * dn_nw_ref[...] * _silu(z)
    mix_in = jnp.concatenate([y_conv, y_dn], axis=1).astype(BF16)
    mix = _dot(mix_in, w_out_ref[...])
    r = DEEP_ALPHA * x + gate1 * mix
    mu = jnp.mean(r, axis=-1, keepdims=True)
    rc = r - mu
    var = jnp.mean(rc * rc, axis=-1, keepdims=True)
    x1 = rc * lax.rsqrt(var + LN_EPS) * ln_g_ref[...] + ln_b_ref[...]
    x1_ref[0] = x1
    h2 = x1 * (1.0 + scale2) + shift2
    _store_rows(h2_ref.at[0], h2)
    logit_ref[0] = _dot(h2.astype(BF16), w_rt_ref[...]) + b_rt_ref[...]


def _const_spec(shape):
    nd = len(shape)
    return pl.BlockSpec(shape, lambda b, j, _nd=nd: (0,) * _nd, pipeline_mode=pl.Buffered(1))


def _mixer_constants(ts):
    m = np.arange(MXU_TILE)
    g64 = (m[:, None] // CONV_GROUP == m[None, :] // CONV_GROUP).astype(np.float32)
    g128 = (m[:, None] // DN_HEAD_DIM == m[None, :] // DN_HEAD_DIM).astype(np.float32)
    i = np.arange(D_DELTA)
    t = np.arange(ts)
    tri = ((t[:, None] // CHUNK == t[None, :] // CHUNK) & (t[None, :] <= t[:, None])).astype(np.float32)
    lane = np.arange(LANES)
    e_beta = (lane[:, None] == i[None, :] // DN_HEAD_DIM).astype(np.float32)
    e_g = (lane[:, None] == DN_HEADS + i[None, :] // DN_HEAD_DIM).astype(np.float32)
    r = np.arange(CHUNK)
    lvl = np.stack([(((r[:, None] >> l) ^ (r[None, :] >> l)) == 1) & (r[:, None] > r[None, :])
                    for l in range(N_LEVELS)]).astype(np.float32)
    as_bf16 = lambda a: jnp.asarray(a, dtype=BF16)
    return as_bf16(g64), as_bf16(g128), as_bf16(tri), as_bf16(e_beta), as_bf16(e_g), jnp.asarray(lvl)


def _mixer(x, mod, w_in, conv_w, conv_norm_w, dn_conv_w, dn_a_log, dn_dt_bias, dn_norm_w, w_out,
           ln_g, ln_b, w_grp, b_grp, w_exp, b_exp):
    n_b, seq, d = x.shape
    ts = SEQ_TILE
    n_main = 3 * D_CONV + 4 * D_DELTA
    w_main = w_in[:, :n_main].astype(BF16)
    w_ba = jnp.pad(w_in[:, n_main:], ((0, 0), (0, LANES - 2 * DN_HEADS))).astype(BF16)
    head_pad = (DN_HEADS, LANES - 2 * DN_HEADS)
    a_row = jnp.pad(dn_a_log, head_pad).reshape(1, LANES)
    dt_row = jnp.pad(dn_dt_bias, head_pad).reshape(1, LANES)
    n_rt = N_GROUPS + N_EXPERTS
    w_rt = jnp.pad(jnp.concatenate([w_grp, w_exp], axis=1), ((0, 0), (0, LANES - n_rt))).astype(BF16)
    b_rt = jnp.pad(jnp.concatenate([b_grp, b_exp]), (0, LANES - n_rt)).reshape(1, LANES)
    consts = _mixer_constants(ts)
    operands = (
        x, mod, w_main, w_ba, conv_w, conv_norm_w.reshape(1, D_CONV), dn_conv_w, a_row, dt_row,
        jnp.tile(dn_norm_w, DN_HEADS).reshape(1, D_DELTA), w_out.astype(BF16),
        ln_g.reshape(1, d), ln_b.reshape(1, d), w_rt, b_rt) + consts
    in_specs = [
        pl.BlockSpec((1, ts, d), lambda b, j: (b, j, 0)),
        pl.BlockSpec((1,) + mod.shape[1:], lambda b, j: (b, 0, 0)),
    ] + [_const_spec(a.shape) for a in operands[2:]]
    tile_spec = lambda width: pl.BlockSpec((1, ts, width), lambda b, j: (b, j, 0))
    return pl.pallas_call(
        _mixer_kernel,
        grid=(n_b, seq // ts),
        in_specs=in_specs,
        out_specs=[tile_spec(d), pl.BlockSpec((1, ts * ROW_LINES, LANES), lambda b, j: (b, j, 0)),
                   tile_spec(LANES)],
        out_shape=[jax.ShapeDtypeStruct((n_b, seq, d), F32),
                   jax.ShapeDtypeStruct((n_b, seq * ROW_LINES, LANES), F32),
                   jax.ShapeDtypeStruct((n_b, seq, LANES), F32)],
        scratch_shapes=[
            pltpu.VMEM((HALO + ts, D_CONV), F32),
            pltpu.VMEM((HALO + ts, 3 * D_DELTA), F32),
            pltpu.VMEM((DN_HEADS, DN_HEAD_DIM, DN_HEAD_DIM), F32),
        ] + [pltpu.VMEM((ts, D_DELTA), F32)] * 7 + [pltpu.VMEM((ts, D_DELTA), BF16)] * 4,
        compiler_params=pltpu.CompilerParams(dimension_semantics=("arbitrary", "arbitrary"),
                                             vmem_limit_bytes=VMEM_LIMIT),
        name="mixer",
    )(*operands)


def _route_kernel(lg_ref, o_ref, cnt_ref):
    lg = lg_ref[...]
    lane = lax.broadcasted_iota(jnp.int32, lg.shape, 1).astype(F32)
    neg = -jnp.inf
    big = float(LANES)
    is_grp = lane < N_GROUPS
    gl = jnp.where(is_grp, lg, neg)
    gm = jnp.max(gl, axis=-1, keepdims=True)
    gi = jnp.min(jnp.where(gl == gm, lane, big), axis=-1, keepdims=True)
    grp_w = 1.0 / jnp.sum(jnp.where(is_grp, jnp.exp(lg - gm), 0.0), axis=-1, keepdims=True)
    lo = N_GROUPS + gi * EXPERTS_PER_GROUP
    el = jnp.where((lane >= lo) & (lane < lo + EXPERTS_PER_GROUP), lg, neg)
    m1 = jnp.max(el, axis=-1, keepdims=True)
    i1 = jnp.min(jnp.where(el == m1, lane, big), axis=-1, keepdims=True)
    el2 = jnp.where(lane == i1, neg, el)
    m2 = jnp.max(el2, axis=-1, keepdims=True)
    i2 = jnp.min(jnp.where(el2 == m2, lane, big), axis=-1, keepdims=True)
    ratio = jnp.exp(m2 - m1)
    g0 = grp_w / (1.0 + ratio)
    g1 = g0 * ratio
    e0 = i1 - N_GROUPS
    e1 = i2 - N_GROUPS
    o_ref[...] = jnp.where(lane == 0, e0, jnp.where(lane == 1, e1, jnp.where(lane == 2, g0,
                                                                            jnp.where(lane == 3, g1, 0.0))))
    chosen = ((lane == e0) | (lane == e1)).astype(F32)
    cnt_ref[...] = jnp.broadcast_to(jnp.sum(chosen, axis=0, keepdims=True), cnt_ref.shape)


def _route(logits):
    n_tok = logits.shape[0]
    n_tiles = n_tok // ROUTE_TILE
    spec = pl.BlockSpec((ROUTE_TILE, LANES), lambda i: (i, 0))
    route, cnt = pl.pallas_call(
        _route_kernel,
        grid=(n_tiles,),
        in_specs=[spec],
        out_specs=[spec, pl.BlockSpec((SUBLANES, LANES), lambda i: (i, 0))],
        out_shape=[jax.ShapeDtypeStruct((n_tok, LANES), F32),
                   jax.ShapeDtypeStruct((n_tiles * SUBLANES, LANES), F32)],
        name="route",
    )(logits)
    counts = jnp.sum(cnt.reshape(n_tiles, SUBLANES, LANES)[:, 0, :N_EXPERTS], axis=0).astype(jnp.int32)
    return route, counts


def _block_table(counts, n_asg):
    padded = ((counts + ROW_BLOCK - 1) // ROW_BLOCK) * ROW_BLOCK
    pend = jnp.cumsum(padded)
    pstart = pend - padded
    n_blocks = n_asg // ROW_BLOCK + N_EXPERTS
    block_start = jnp.arange(n_blocks, dtype=jnp.int32) * ROW_BLOCK
    block_expert = jnp.minimum(jnp.sum(pend[None, :] <= block_start[:, None], axis=1),
                               N_EXPERTS - 1).astype(jnp.int32)
    n_valid = jnp.clip((pstart + counts)[block_expert] - block_start, 0, ROW_BLOCK).astype(jnp.int32)
    n_used = (pend[-1:] // ROW_BLOCK).astype(jnp.int32)
    return pstart, block_expert, n_valid, n_used


def _pos_kernel(route_ref, pstart_ref, tri_ref, pos_ref, running):
    @pl.when(pl.program_id(0) == 0)
    def _():
        running[...] = jnp.zeros(running.shape, F32)

    rt = route_ref[...]
    lane = lax.broadcasted_iota(jnp.int32, rt.shape, 1).astype(F32)
    oh0 = (lane == rt[:, 0:1]).astype(F32)
    oh1 = (lane == rt[:, 1:2]).astype(F32)
    both = oh0 + oh1
    before = pstart_ref[...] + running[0:1, :] + _dot(tri_ref[...], both.astype(BF16))
    pos0 = jnp.sum(oh0 * before, axis=-1, keepdims=True)
    pos1 = jnp.sum(oh1 * before, axis=-1, keepdims=True)
    pos_ref[...] = jnp.where(lane == 0, pos0, jnp.where(lane == 1, pos1, 0.0)).astype(jnp.int32)
    running[...] = running[...] + jnp.sum(both, axis=0, keepdims=True)


def _positions(route, pstart):
    n_tok = route.shape[0]
    t = np.arange(ROUTE_TILE)
    tri = jnp.asarray((t[None, :] < t[:, None]).astype(np.float32), dtype=BF16)
    pstart_row = jnp.pad(pstart.astype(F32), (0, LANES - N_EXPERTS)).reshape(1, LANES)
    spec = pl.BlockSpec((ROUTE_TILE, LANES), lambda i: (i, 0))
    return pl.pallas_call(
        _pos_kernel,
        grid=(n_tok // ROUTE_TILE,),
        in_specs=[spec, pl.BlockSpec((1, LANES), lambda i: (0, 0)),
                  pl.BlockSpec((ROUTE_TILE, ROUTE_TILE), lambda i: (0, 0), pipeline_mode=pl.Buffered(1))],
        out_specs=spec,
        out_shape=jax.ShapeDtypeStruct((n_tok, LANES), jnp.int32),
        scratch_shapes=[pltpu.VMEM((SUBLANES, LANES), F32)],
        compiler_params=pltpu.CompilerParams(dimension_semantics=("arbitrary",)),
        name="positions",
    )(route, pstart_row, tri)


def _dispatch_kernel(nv_ref, pos_ref, h2_ref, x_hbm, zbuf, sem_fill, sem_rows):
    i = pl.program_id(0)
    n_blocks = nv_ref.shape[0]
    n_tok_step = pos_ref.shape[2] // TOP_K

    def fill(b):
        return pltpu.make_async_copy(zbuf, x_hbm.at[_lines(b * ROW_BLOCK, ROW_BLOCK)], sem_fill)

    @pl.when(i == 0)
    def _():
        zbuf[...] = jnp.zeros(zbuf.shape, F32)

        def start_fill(b, carry):
            @pl.when(nv_ref[b] < ROW_BLOCK)
            def _():
                fill(b).start()
            return carry

        def wait_fill(b, carry):
            @pl.when(nv_ref[b] < ROW_BLOCK)
            def _():
                fill(b).wait()
            return carry

        lax.fori_loop(0, n_blocks, start_fill, 0)
        lax.fori_loop(0, n_blocks, wait_fill, 0)

    def group(g, carry):
        for j in range(DMA_UNROLL):
            t = g * DMA_UNROLL + j
            src = h2_ref.at[_lines(t)]
            for k in range(TOP_K):
                pltpu.make_async_copy(src, x_hbm.at[_lines(pos_ref[0, 0, t * TOP_K + k])],
                                      sem_rows).start(priority=k)
        return carry

    lax.fori_loop(0, n_tok_step // DMA_UNROLL, group, 0)
    for _ in range(TOP_K):
        pltpu.make_async_copy(h2_ref, x_hbm.at[pl.ds(0, h2_ref.shape[0])], sem_rows).wait()


def _dispatch(h2_lines, pos_steps, n_valid):
    n_blocks = n_valid.shape[0]
    step_lines = pos_steps.shape[2] // TOP_K * ROW_LINES
    grid_spec = pltpu.PrefetchScalarGridSpec(
        num_scalar_prefetch=1,
        grid=(pos_steps.shape[0],),
        in_specs=[pl.BlockSpec((1, 1, pos_steps.shape[2]), lambda i, nv: (i, 0, 0), memory_space=pltpu.SMEM),
                  pl.BlockSpec((step_lines, LANES), lambda i, nv: (i, 0))],
        out_specs=pl.BlockSpec(memory_space=pl.ANY),
        scratch_shapes=[pltpu.VMEM((ROW_BLOCK * ROW_LINES, LANES), F32),
                        pltpu.SemaphoreType.DMA(()), pltpu.SemaphoreType.DMA(())],
    )
    return pl.pallas_call(
        _dispatch_kernel,
        grid_spec=grid_spec,
        out_shape=jax.ShapeDtypeStruct((n_blocks * ROW_BLOCK * ROW_LINES, LANES), F32),
        compiler_params=pltpu.CompilerParams(dimension_semantics=("arbitrary",)),
        name="dispatch",
    )(n_valid, pos_steps, h2_lines)


def _expert_kernel(be_ref, nu_ref, x_ref, wg_ref, wu_ref, wd_ref, y_ref):
    del be_ref
    used = pl.program_id(0) < nu_ref[0]

    @pl.when(used)
    def _():
        xb = _load_rows(x_ref, ROW_BLOCK).astype(BF16)
        gate = _dot(xb, wg_ref[0])
        up = _dot(xb, wu_ref[0])
        _store_rows(y_ref, _dot((_silu(gate) * up).astype(BF16), wd_ref[0]))

    @pl.when(jnp.logical_not(used))
    def _():
        y_ref[...] = jnp.zeros(y_ref.shape, F32)


def _experts(x_lines, block_expert, n_used, w_gate, w_up, w_down):
    n_blocks = block_expert.shape[0]
    d = w_gate.shape[1]
    block_lines = ROW_BLOCK * ROW_LINES
    last_used = lambda i, nu: jnp.minimum(i, jnp.maximum(nu[0] - 1, 0))
    w_spec = lambda shape: pl.BlockSpec(shape, lambda i, be, nu: (be[last_used(i, nu)], 0, 0))
    grid_spec = pltpu.PrefetchScalarGridSpec(
        num_scalar_prefetch=2,
        grid=(n_blocks,),
        in_specs=[
            pl.BlockSpec((block_lines, LANES), lambda i, be, nu: (last_used(i, nu), 0)),
            w_spec((1, d, D_EXPERT)), w_spec((1, d, D_EXPERT)), w_spec((1, D_EXPERT, d)),
        ],
        out_specs=pl.BlockSpec((block_lines, LANES), lambda i, be, nu: (i, 0)),
    )
    return pl.pallas_call(
        _expert_kernel,
        grid_spec=grid_spec,
        out_shape=jax.ShapeDtypeStruct(x_lines.shape, F32),
        compiler_params=pltpu.CompilerParams(dimension_semantics=("arbitrary",)),
        name="experts",
    )(block_expert, n_used, x_lines, w_gate.astype(BF16), w_up.astype(BF16), w_down.astype(BF16))


def _final_kernel(pos_ref, pos_next_ref, x1_ref, mod_ref, route_ref, ln_g_ref, ln_b_ref, y_hbm, o_ref,
                  ybuf, sem):
    i = pl.program_id(0)
    slot = i % 2
    tt = x1_ref.shape[1]

    def start_gather(idx_ref, to_slot):
        def group(g, carry):
            for j in range(DMA_UNROLL):
                t = g * DMA_UNROLL + j
                for k in range(TOP_K):
                    pltpu.make_async_copy(y_hbm.at[_lines(idx_ref[0, 0, t * TOP_K + k])],
                                          ybuf.at[to_slot, _lines(k * tt + t)],
                                          sem.at[to_slot]).start(priority=k)
            return carry
        lax.fori_loop(0, tt // DMA_UNROLL, group, 0)

    @pl.when(i == 0)
    def _():
        start_gather(pos_ref, 0)

    @pl.when(i < pl.num_programs(0) - 1)
    def _():
        start_gather(pos_next_ref, 1 - slot)

    pltpu.make_async_copy(y_hbm.at[pl.ds(0, ybuf.shape[1])], ybuf.at[slot], sem.at[slot]).wait()
    x1 = x1_ref[0]
    gate2 = mod_ref[0, 5:6, :]
    rt = route_ref[...]
    rows = ybuf.at[slot]
    y0 = _load_rows(rows.at[pl.ds(0, tt * ROW_LINES)], tt)
    y1 = _load_rows(rows.at[pl.ds(tt * ROW_LINES, tt * ROW_LINES)], tt)
    r = DEEP_ALPHA * x1 + gate2 * (rt[:, 2:3] * y0 + rt[:, 3:4] * y1)
    mu = jnp.mean(r, axis=-1, keepdims=True)
    rc = r - mu
    var = jnp.mean(rc * rc, axis=-1, keepdims=True)
    o_ref[0] = rc * lax.rsqrt(var + LN_EPS) * ln_g_ref[...] + ln_b_ref[...]


def _final(x1, mod, y_lines, route, pos_steps, ln_g, ln_b):
    n_b, seq, d = x1.shape
    tt = FINAL_TILE
    per_b = seq // tt
    n_steps = n_b * per_b
    pos_spec = lambda shift: pl.BlockSpec((1, 1, TOP_K * tt), lambda i: (jnp.minimum(i + shift, n_steps - 1), 0, 0),
                                          memory_space=pltpu.SMEM)
    return pl.pallas_call(
        _final_kernel,
        grid=(n_steps,),
        in_specs=[
            pos_spec(0), pos_spec(1),
            pl.BlockSpec((1, tt, d), lambda i: (i // per_b, i % per_b, 0)),
            pl.BlockSpec((1,) + mod.shape[1:], lambda i: (i // per_b, 0, 0)),
            pl.BlockSpec((tt, LANES), lambda i: (i, 0)),
            pl.BlockSpec((1, d), lambda i: (0, 0)),
            pl.BlockSpec((1, d), lambda i: (0, 0)),
            pl.BlockSpec(memory_space=pl.ANY),
        ],
        out_specs=pl.BlockSpec((1, tt, d), lambda i: (i // per_b, i % per_b, 0)),
        out_shape=jax.ShapeDtypeStruct((n_b, seq, d), F32),
        scratch_shapes=[pltpu.VMEM((2, TOP_K * tt * ROW_LINES, LANES), F32), pltpu.SemaphoreType.DMA((2,))],
        compiler_params=pltpu.CompilerParams(dimension_semantics=("arbitrary",), vmem_limit_bytes=VMEM_LIMIT),
        name="final",
    )(pos_steps, pos_steps, x1, mod, route, ln_g.reshape(1, d), ln_b.reshape(1, d), y_lines)


def kernel(x, c, w_ada, b_ada, w_in, conv_w, conv_norm_w, dn_conv_w, dn_A_log, dn_dt_bias, dn_norm_w,
           w_out, ln1_g, ln1_b, w_grp, b_grp, w_exp, b_exp, w_gate, w_up, w_down, ln2_g, ln2_b):
    n_b, seq, d = x.shape
    n_tok = n_b * seq
    for l in range(w_ada.shape[0]):
        mod = _ada(c, w_ada[l], b_ada[l].reshape(1, -1)).reshape(n_b, 6, d)
        x1, h2, logits = _mixer(x, mod, w_in[l], conv_w[l], conv_norm_w[l], dn_conv_w[l], dn_A_log[l],
                                dn_dt_bias[l], dn_norm_w[l], w_out[l], ln1_g[l], ln1_b[l],
                                w_grp[l], b_grp[l], w_exp[l], b_exp[l])
        route, counts = _route(logits.reshape(n_tok, LANES))
        pstart, block_expert, n_valid, n_used = _block_table(counts, n_tok * TOP_K)
        pos = _positions(route, pstart)[:, 0:TOP_K]
        pos_steps = pos.reshape(n_tok // FINAL_TILE, 1, TOP_K * FINAL_TILE)
        x_rows = _dispatch(h2.reshape(n_tok * ROW_LINES, LANES), pos_steps, n_valid)
        y_rows = _experts(x_rows, block_expert, n_used, w_gate[l], w_up[l], w_down[l])
        x = _final(x1, mod, y_rows, route, pos_steps, ln2_g[l], ln2_b[l])
    return x
```

```python
import functools

import numpy as np
import jax
import jax.numpy as jnp
from jax import lax
from jax.experimental import pallas as pl
from jax.experimental.pallas import tpu as pltpu

F32 = jnp.float32
BF16 = jnp.bfloat16

D_MODEL = 1024
D_CONV = 512
CONV_GROUP = 64
DN_HEADS = 4
DN_HEAD_DIM = 128
D_DELTA = DN_HEADS * DN_HEAD_DIM
N_GROUPS = 4
EXPERTS_PER_GROUP = 8
N_EXPERTS = N_GROUPS * EXPERTS_PER_GROUP
TOP_K = 2
D_EXPERT = 256
LN_EPS = 1e-5
RMS_EPS = 1e-6
DEEP_ALPHA = 2.0 ** 0.25

LANES = 128
MXU_TILE = 256
SUBLANES = 8
HALO = SUBLANES
SEQ_TILE = 512
CHUNK = 128
N_LEVELS = 7
ROUTE_TILE = 1024
ROW_BLOCK = 256
ROW_LINES = D_MODEL // LANES
DMA_UNROLL = 8
FINAL_TILE = 512
VMEM_LIMIT = 56 * 1024 * 1024


def _dot(a, b):
    return jnp.dot(a, b, preferred_element_type=F32)


def _dot_nt(a, b):
    return lax.dot_general(a, b, (((1,), (1,)), ((), ())), preferred_element_type=F32)


def _split(x):
    hi = x.astype(BF16)
    lo = (x - hi.astype(F32)).astype(BF16)
    return hi, lo


def _dot_split_lhs(x, m):
    hi, lo = _split(x)
    return _dot(hi, m) + _dot(lo, m)


def _group_sums(x, g):
    w = g.shape[0]
    return jnp.concatenate([_dot_split_lhs(x[:, i:i + w], g) for i in range(0, x.shape[1], w)], axis=1)


def _head_sums(x):
    w = DN_HEAD_DIM
    return jnp.concatenate([jnp.broadcast_to(jnp.sum(x[:, c:c + w], axis=-1, keepdims=True), (x.shape[0], w))
                            for c in range(0, x.shape[1], w)], axis=1)


def _head_bcast(cols, first):
    return jnp.concatenate([jnp.broadcast_to(cols[:, first + h:first + h + 1], (cols.shape[0], DN_HEAD_DIM))
                            for h in range(DN_HEADS)], axis=1)


def _dot_split_rhs(m, x):
    hi, lo = _split(x)
    return _dot(m, hi) + _dot(m, lo)


def _sigmoid(x):
    return 1.0 / (1.0 + jnp.exp(-x))


def _silu(x):
    return x * _sigmoid(x)


def _load_rows(ref, n_rows):
    return jnp.concatenate([ref[pl.ds(s, n_rows, stride=ROW_LINES), :] for s in range(ROW_LINES)], axis=1)


def _store_rows(ref, val):
    for s in range(ROW_LINES):
        ref[pl.ds(s, val.shape[0], stride=ROW_LINES), :] = val[:, s * LANES:(s + 1) * LANES]


def _lines(row, n=1):
    return pl.ds(pl.multiple_of(row * ROW_LINES, ROW_LINES), n * ROW_LINES)


def _ada_kernel(c_ref, w_ref, b_ref, o_ref):
    c = c_ref[...]
    o_ref[...] = _dot(_silu(c).astype(BF16), w_ref[...].astype(BF16)) + b_ref[...]


def _ada(c, w, b):
    n_b, d = c.shape
    n_out = w.shape[1]
    return pl.pallas_call(
        _ada_kernel,
        grid=(n_out // d,),
        in_specs=[
            pl.BlockSpec((n_b, d), lambda i: (0, 0)),
            pl.BlockSpec((d, d), lambda i: (0, i)),
            pl.BlockSpec((1, d), lambda i: (0, i)),
        ],
        out_specs=pl.BlockSpec((n_b, d), lambda i: (0, i)),
        out_shape=jax.ShapeDtypeStruct((n_b, n_out), F32),
        name="ada",
    )(c, w, b)


def _shifted(full, k):
    return pltpu.roll(full, k, axis=0)[HALO:]


def _mixer_kernel(x_ref, mod_ref, w_main_ref, w_ba_ref, conv_w_ref, conv_nw_ref, dn_conv_w_ref,
                  a_row_ref, dt_row_ref, dn_nw_ref, w_out_ref, ln_g_ref, ln_b_ref, w_rt_ref, b_rt_ref,
                  g64_ref, tri_ref, lvl_ref,
                  x1_ref, h2_ref, logit_ref,
                  cu_buf, qkv_buf, s_ref, q_s, k_s, v_s, o_s, beta_s, gc_s, u_s,
                  w_s, qkg_s, qd_s, kend_t_s):
    ts = x_ref.shape[1]

    @pl.when(pl.program_id(1) == 0)
    def _():
        cu_buf[0:HALO, :] = jnp.zeros((HALO, D_CONV), F32)
        qkv_buf[0:HALO, :] = jnp.zeros((HALO, 3 * D_DELTA), F32)
        s_ref[...] = jnp.zeros(s_ref.shape, F32)

    x = x_ref[0]
    shift1 = mod_ref[0, 0:1, :]
    scale1 = mod_ref[0, 1:2, :]
    gate1 = mod_ref[0, 2:3, :]
    shift2 = mod_ref[0, 3:4, :]
    scale2 = mod_ref[0, 4:5, :]
    h = (x * (1.0 + scale1) + shift1).astype(BF16)

    b_gate = _dot(h, w_main_ref[:, 0:D_CONV])
    c_gate = _dot(h, w_main_ref[:, D_CONV:2 * D_CONV])
    u_in = _dot(h, w_main_ref[:, 2 * D_CONV:3 * D_CONV])
    cu = c_gate * u_in
    cu_buf[HALO:HALO + ts, :] = cu
    full = cu_buf[...]
    cw = conv_w_ref[...]
    conv = cw[0:1, :] * _shifted(full, 2) + cw[1:2, :] * _shifted(full, 1) + cw[2:3, :] * cu
    cu_buf[0:HALO, :] = cu[ts - HALO:ts, :]
    y = b_gate * conv
    ms = _group_sums(y * y, g64_ref[...]) * (1.0 / CONV_GROUP)
    y_conv = y * lax.rsqrt(ms + RMS_EPS) * conv_nw_ref[...]

    off = 3 * D_CONV
    qkv = _dot(h, w_main_ref[:, off:off + 3 * D_DELTA])
    qkv_buf[HALO:HALO + ts, :] = qkv
    fullq = qkv_buf[...]
    w4 = dn_conv_w_ref[...]
    acc = (w4[0:1, :] * _shifted(fullq, 3) + w4[1:2, :] * _shifted(fullq, 2)
           + w4[2:3, :] * _shifted(fullq, 1) + w4[3:4, :] * qkv)
    qkv_buf[0:HALO, :] = qkv[ts - HALO:ts, :]
    act = _silu(acc)
    q = act[:, 0:D_DELTA]
    k = act[:, D_DELTA:2 * D_DELTA]
    q_s[...] = q * lax.rsqrt(_head_sums(q * q) + RMS_EPS) * (DN_HEAD_DIM ** -0.5)
    k_s[...] = k * lax.rsqrt(_head_sums(k * k) + RMS_EPS)
    v_s[...] = act[:, 2 * D_DELTA:3 * D_DELTA]

    ba = _dot(h, w_ba_ref[...])
    beta_all = _sigmoid(ba)
    a_in = ba + dt_row_ref[...]
    softplus = jnp.maximum(a_in, 0.0) + jnp.log1p(jnp.exp(-jnp.abs(a_in)))
    g_all = -jnp.exp(a_row_ref[...]) * softplus
    gc_all = _dot_split_rhs(tri_ref[...], g_all)
    beta_s[...] = _head_bcast(beta_all, 0)
    gc_s[...] = _head_bcast(gc_all, DN_HEADS)

    row = lax.broadcasted_iota(jnp.int32, (CHUNK, CHUNK), 0)
    col = lax.broadcasted_iota(jnp.int32, (CHUNK, CHUNK), 1)
    incl = row >= col
    eye = (row == col).astype(F32)
    blocks = [(slice(c * CHUNK, (c + 1) * CHUNK), slice(hd * DN_HEAD_DIM, (hd + 1) * DN_HEAD_DIM))
              for c in range(ts // CHUNK) for hd in range(DN_HEADS)]

    lmats, t_invs = [], []
    for rows, cols in blocks:
        qh = q_s[rows, cols]
        kh = k_s[rows, cols]
        gc = gc_s[rows, cols]
        gam = jnp.exp(jnp.where(incl, gc - gc.T, -jnp.inf))
        kb = kh.astype(BF16)
        lmat = beta_s[rows, cols] * _dot_nt(kb, kb) * gam
        qkg_s[rows, cols] = (_dot_nt(qh.astype(BF16), kb) * gam).astype(BF16)
        eg = jnp.exp(gc)
        qd_s[rows, cols] = (qh * eg).astype(BF16)
        k_end = kh * jnp.exp(gc[CHUNK - 1:CHUNK, :] - gc)
        kend_t_s[rows, cols] = k_end.T.astype(BF16)
        lmats.append(lmat)
        t_invs.append(eye - lmat * lvl_ref[0])
    for lv in range(1, N_LEVELS):
        tbs = [t.astype(BF16) for t in t_invs]
        ys = [_dot((lm * lvl_ref[lv]).astype(BF16), tb).astype(BF16) for lm, tb in zip(lmats, tbs)]
        t_invs = [t - _dot(tb, y) for t, tb, y in zip(t_invs, tbs, ys)]
    for (rows, cols), t_inv in zip(blocks, t_invs):
        beta = beta_s[rows, cols]
        kh = k_s[rows, cols]
        rhs = jnp.concatenate([beta * v_s[rows, cols], beta * jnp.exp(gc_s[rows, cols]) * kh], axis=1)
        sol = _dot(t_inv.astype(BF16), rhs.astype(BF16))
        u_s[rows, cols] = sol[:, 0:DN_HEAD_DIM]
        w_s[rows, cols] = sol[:, DN_HEAD_DIM:2 * DN_HEAD_DIM].astype(BF16)

    heads = [slice(hd * DN_HEAD_DIM, (hd + 1) * DN_HEAD_DIM) for hd in range(DN_HEADS)]
    for c in range(ts // CHUNK):
        rows = slice(c * CHUNK, (c + 1) * CHUNK)
        states = [s_ref[hd] for hd in range(DN_HEADS)]
        sbs = [st.astype(BF16) for st in states]
        dbs = [(u_s[rows, cols] - _dot(w_s[rows, cols], sb)).astype(BF16) for cols, sb in zip(heads, sbs)]
        for hd, cols in enumerate(heads):
            o_s[rows, cols] = _dot(qd_s[rows, cols], sbs[hd]) + _dot(qkg_s[rows, cols], dbs[hd])
            decay = jnp.exp(gc_s[(c + 1) * CHUNK - 1:(c + 1) * CHUNK, cols])
            s_ref[hd] = states[hd] * decay + _dot(kend_t_s[rows, cols], dbs[hd])

    o = o_s[...]
    z = _dot(h, w_main_ref[:, off + 3 * D_DELTA:off + 4 * D_DELTA])
    o_ms = _head_sums(o * o) * (1.0 / DN_HEAD_DIM)
    y_dn = o * lax.rsqrt(o_ms + RMS_EPS) * dn_nw_ref[...] * _silu(z)
    mix_in = jnp.concatenate([y_conv, y_dn], axis=1).astype(BF16)
    mix = _dot(mix_in, w_out_ref[...])
    r = DEEP_ALPHA * x + gate1 * mix
    mu = jnp.mean(r, axis=-1, keepdims=True)
    rc = r - mu
    var = jnp.mean(rc * rc, axis=-1, keepdims=True)
    x1 = rc * lax.rsqrt(var + LN_EPS) * ln_g_ref[...] + ln_b_ref[...]
    x1_ref[0] = x1
    h2 = x1 * (1.0 + scale2) + shift2
    _store_rows(h2_ref.at[0], h2)
    logit_ref[0] = _dot(h2.astype(BF16), w_rt_ref[...]) + b_rt_ref[...]


def _const_spec(shape):
    nd = len(shape)
    return pl.BlockSpec(shape, lambda b, j, _nd=nd: (0,) * _nd, pipeline_mode=pl.Buffered(1))


def _mixer_constants(ts):
    m = np.arange(MXU_TILE)
    g64 = (m[:, None] // CONV_GROUP == m[None, :] // CONV_GROUP).astype(np.float32)
    t = np.arange(ts)
    tri = ((t[:, None] // CHUNK == t[None, :] // CHUNK) & (t[None, :] <= t[:, None])).astype(np.float32)
    r = np.arange(CHUNK)
    lvl = np.stack([(((r[:, None] >> l) ^ (r[None, :] >> l)) == 1) & (r[:, None] > r[None, :])
                    for l in range(N_LEVELS)]).astype(np.float32)
    as_bf16 = lambda a: jnp.asarray(a, dtype=BF16)
    return as_bf16(g64), as_bf16(tri), jnp.asarray(lvl)


def _mixer(x, mod, w_in, conv_w, conv_norm_w, dn_conv_w, dn_a_log, dn_dt_bias, dn_norm_w, w_out,
           ln_g, ln_b, w_grp, b_grp, w_exp, b_exp):
    n_b, seq, d = x.shape
    ts = SEQ_TILE
    n_main = 3 * D_CONV + 4 * D_DELTA
    w_main = w_in[:, :n_main].astype(BF16)
    w_ba = jnp.pad(w_in[:, n_main:], ((0, 0), (0, LANES - 2 * DN_HEADS))).astype(BF16)
    head_pad = (DN_HEADS, LANES - 2 * DN_HEADS)
    a_row = jnp.pad(dn_a_log, head_pad).reshape(1, LANES)
    dt_row = jnp.pad(dn_dt_bias, head_pad).reshape(1, LANES)
    n_rt = N_GROUPS + N_EXPERTS
    w_rt = jnp.pad(jnp.concatenate([w_grp, w_exp], axis=1), ((0, 0), (0, LANES - n_rt))).astype(BF16)
    b_rt = jnp.pad(jnp.concatenate([b_grp, b_exp]), (0, LANES - n_rt)).reshape(1, LANES)
    consts = _mixer_constants(ts)
    operands = (
        x, mod, w_main, w_ba, conv_w, conv_norm_w.reshape(1, D_CONV), dn_conv_w, a_row, dt_row,
        jnp.tile(dn_norm_w, DN_HEADS).reshape(1, D_DELTA), w_out.astype(BF16),
        ln_g.reshape(1, d), ln_b.reshape(1, d), w_rt, b_rt) + consts
    in_specs = [
        pl.BlockSpec((1, ts, d), lambda b, j: (b, j, 0)),
        pl.BlockSpec((1,) + mod.shape[1:], lambda b, j: (b, 0, 0)),
    ] + [_const_spec(a.shape) for a in operands[2:]]
    tile_spec = lambda width: pl.BlockSpec((1, ts, width), lambda b, j: (b, j, 0))
    return pl.pallas_call(
        _mixer_kernel,
        grid=(n_b, seq // ts),
        in_specs=in_specs,
        out_specs=[tile_spec(d), pl.BlockSpec((1, ts * ROW_LINES, LANES), lambda b, j: (b, j, 0)),
                   tile_spec(LANES)],
        out_shape=[jax.ShapeDtypeStruct((n_b, seq, d), F32),
                   jax.ShapeDtypeStruct((n_b, seq * ROW_LINES, LANES), F32),
                   jax.ShapeDtypeStruct((n_b, seq, LANES), F32)],
        scratch_shapes=[
            pltpu.VMEM((HALO + ts, D_CONV), F32),
            pltpu.VMEM((HALO + ts, 3 * D_DELTA), F32),
            pltpu.VMEM((DN_HEADS, DN_HEAD_DIM, DN_HEAD_DIM), F32),
        ] + [pltpu.VMEM((ts, D_DELTA), F32)] * 7 + [pltpu.VMEM((ts, D_DELTA), BF16)] * 4,
        compiler_params=pltpu.CompilerParams(dimension_semantics=("arbitrary", "arbitrary"),
                                             vmem_limit_bytes=VMEM_LIMIT),
        name="mixer",
    )(*operands)


def _route_kernel(lg_ref, o_ref, cnt_ref):
    lg = lg_ref[...]
    lane = lax.broadcasted_iota(jnp.int32, lg.shape, 1).astype(F32)
    neg = -jnp.inf
    big = float(LANES)
    is_grp = lane < N_GROUPS
    gl = jnp.where(is_grp, lg, neg)
    gm = jnp.max(gl, axis=-1, keepdims=True)
    gi = jnp.min(jnp.where(gl == gm, lane, big), axis=-1, keepdims=True)
    grp_w = 1.0 / jnp.sum(jnp.where(is_grp, jnp.exp(lg - gm), 0.0), axis=-1, keepdims=True)
    lo = N_GROUPS + gi * EXPERTS_PER_GROUP
    el = jnp.where((lane >= lo) & (lane < lo + EXPERTS_PER_GROUP), lg, neg)
    m1 = jnp.max(el, axis=-1, keepdims=True)
    i1 = jnp.min(jnp.where(el == m1, lane, big), axis=-1, keepdims=True)
    el2 = jnp.where(lane == i1, neg, el)
    m2 = jnp.max(el2, axis=-1, keepdims=True)
    i2 = jnp.min(jnp.where(el2 == m2, lane, big), axis=-1, keepdims=True)
    ratio = jnp.exp(m2 - m1)
    g0 = grp_w / (1.0 + ratio)
    g1 = g0 * ratio
    e0 = i1 - N_GROUPS
    e1 = i2 - N_GROUPS
    o_ref[...] = jnp.where(lane == 0, e0, jnp.where(lane == 1, e1, jnp.where(lane == 2, g0,
                                                                            jnp.where(lane == 3, g1, 0.0))))
    chosen = ((lane == e0) | (lane == e1)).astype(F32)
    cnt_ref[...] = jnp.broadcast_to(jnp.sum(chosen, axis=0, keepdims=True), cnt_ref.shape)


def _route(logits):
    n_tok = logits.shape[0]
    n_tiles = n_tok // ROUTE_TILE
    spec = pl.BlockSpec((ROUTE_TILE, LANES), lambda i: (i, 0))
    route, cnt = pl.pallas_call(
        _route_kernel,
        grid=(n_tiles,),
        in_specs=[spec],
        out_specs=[spec, pl.BlockSpec((SUBLANES, LANES), lambda i: (i, 0))],
        out_shape=[jax.ShapeDtypeStruct((n_tok, LANES), F32),
                   jax.ShapeDtypeStruct((n_tiles * SUBLANES, LANES), F32)],
        name="route",
    )(logits)
    counts = jnp.sum(cnt.reshape(n_tiles, SUBLANES, LANES)[:, 0, :N_EXPERTS], axis=0).astype(jnp.int32)
    return route, counts


def _block_table(counts, n_asg):
    padded = ((counts + ROW_BLOCK - 1) // ROW_BLOCK) * ROW_BLOCK
    pend = jnp.cumsum(padded)
    pstart = pend - padded
    n_blocks = n_asg // ROW_BLOCK + N_EXPERTS
    block_start = jnp.arange(n_blocks, dtype=jnp.int32) * ROW_BLOCK
    block_expert = jnp.minimum(jnp.sum(pend[None, :] <= block_start[:, None], axis=1),
                               N_EXPERTS - 1).astype(jnp.int32)
    n_valid = jnp.clip((pstart + counts)[block_expert] - block_start, 0, ROW_BLOCK).astype(jnp.int32)
    n_used = (pend[-1:] // ROW_BLOCK).astype(jnp.int32)
    return pstart, block_expert, n_valid, n_used


def _pos_kernel(route_ref, pstart_ref, tri_ref, pos_ref, running):
    @pl.when(pl.program_id(0) == 0)
    def _():
        running[...] = jnp.zeros(running.shape, F32)

    rt = route_ref[...]
    lane = lax.broadcasted_iota(jnp.int32, rt.shape, 1).astype(F32)
    oh0 = (lane == rt[:, 0:1]).astype(F32)
    oh1 = (lane == rt[:, 1:2]).astype(F32)
    both = oh0 + oh1
    before = pstart_ref[...] + running[0:1, :] + _dot(tri_ref[...], both.astype(BF16))
    pos0 = jnp.sum(oh0 * before, axis=-1, keepdims=True)
    pos1 = jnp.sum(oh1 * before, axis=-1, keepdims=True)
    pos_ref[...] = jnp.where(lane == 0, pos0, jnp.where(lane == 1, pos1, 0.0)).astype(jnp.int32)
    running[...] = running[...] + jnp.sum(both, axis=0, keepdims=True)


def _positions(route, pstart):
    n_tok = route.shape[0]
    t = np.arange(ROUTE_TILE)
    tri = jnp.asarray((t[None, :] < t[:, None]).astype(np.float32), dtype=BF16)
    pstart_row = jnp.pad(pstart.astype(F32), (0, LANES - N_EXPERTS)).reshape(1, LANES)
    spec = pl.BlockSpec((ROUTE_TILE, LANES), lambda i: (i, 0))
    return pl.pallas_call(
        _pos_kernel,
        grid=(n_tok // ROUTE_TILE,),
        in_specs=[spec, pl.BlockSpec((1, LANES), lambda i: (0, 0)),
                  pl.BlockSpec((ROUTE_TILE, ROUTE_TILE), lambda i: (0, 0), pipeline_mode=pl.Buffered(1))],
        out_specs=spec,
        out_shape=jax.ShapeDtypeStruct((n_tok, LANES), jnp.int32),
        scratch_shapes=[pltpu.VMEM((SUBLANES, LANES), F32)],
        compiler_params=pltpu.CompilerParams(dimension_semantics=("arbitrary",)),
        name="positions",
    )(route, pstart_row, tri)


def _dispatch_kernel(nv_ref, pos_ref, h2_ref, x_hbm, zbuf, sem_fill, sem_rows):
    i = pl.program_id(0)
    n_blocks = nv_ref.shape[0]
    n_tok_step = pos_ref.shape[2] // TOP_K

    def fill(b):
        return pltpu.make_async_copy(zbuf, x_hbm.at[_lines(b * ROW_BLOCK, ROW_BLOCK)], sem_fill)

    @pl.when(i == 0)
    def _():
        zbuf[...] = jnp.zeros(zbuf.shape, F32)

        def start_fill(b, carry):
            @pl.when(nv_ref[b] < ROW_BLOCK)
            def _():
                fill(b).start()
            return carry

        def wait_fill(b, carry):
            @pl.when(nv_ref[b] < ROW_BLOCK)
            def _():
                fill(b).wait()
            return carry

        lax.fori_loop(0, n_blocks, start_fill, 0)
        lax.fori_loop(0, n_blocks, wait_fill, 0)

    def group(g, carry):
        for j in range(DMA_UNROLL):
            t = g * DMA_UNROLL + j
            src = h2_ref.at[_lines(t)]
            for k in range(TOP_K):
                pltpu.make_async_copy(src, x_hbm.at[_lines(pos_ref[0, 0, t * TOP_K + k])],
                                      sem_rows).start(priority=k)
        return carry

    lax.fori_loop(0, n_tok_step // DMA_UNROLL, group, 0)
    for _ in range(TOP_K):
        pltpu.make_async_copy(h2_ref, x_hbm.at[pl.ds(0, h2_ref.shape[0])], sem_rows).wait()


def _dispatch(h2_lines, pos_steps, n_valid):
    n_blocks = n_valid.shape[0]
    step_lines = pos_steps.shape[2] // TOP_K * ROW_LINES
    grid_spec = pltpu.PrefetchScalarGridSpec(
        num_scalar_prefetch=1,
        grid=(pos_steps.shape[0],),
        in_specs=[pl.BlockSpec((1, 1, pos_steps.shape[2]), lambda i, nv: (i, 0, 0), memory_space=pltpu.SMEM),
                  pl.BlockSpec((step_lines, LANES), lambda i, nv: (i, 0))],
        out_specs=pl.BlockSpec(memory_space=pl.ANY),
        scratch_shapes=[pltpu.VMEM((ROW_BLOCK * ROW_LINES, LANES), F32),
                        pltpu.SemaphoreType.DMA(()), pltpu.SemaphoreType.DMA(())],
    )
    return pl.pallas_call(
        _dispatch_kernel,
        grid_spec=grid_spec,
        out_shape=jax.ShapeDtypeStruct((n_blocks * ROW_BLOCK * ROW_LINES, LANES), F32),
        compiler_params=pltpu.CompilerParams(dimension_semantics=("arbitrary",)),
        name="dispatch",
    )(n_valid, pos_steps, h2_lines)


def _expert_kernel(be_ref, nu_ref, x_ref, wg_ref, wu_ref, wd_ref, y_ref):
    del be_ref
    used = pl.program_id(0) < nu_ref[0]

    @pl.when(used)
    def _():
        xb = _load_rows(x_ref, ROW_BLOCK).astype(BF16)
        gate = _dot(xb, wg_ref[0])
        up = _dot(xb, wu_ref[0])
        _store_rows(y_ref, _dot((_silu(gate) * up).astype(BF16), wd_ref[0]))

    @pl.when(jnp.logical_not(used))
    def _():
        y_ref[...] = jnp.zeros(y_ref.shape, F32)


def _experts(x_lines, block_expert, n_used, w_gate, w_up, w_down):
    n_blocks = block_expert.shape[0]
    d = w_gate.shape[1]
    block_lines = ROW_BLOCK * ROW_LINES
    last_used = lambda i, nu: jnp.minimum(i, jnp.maximum(nu[0] - 1, 0))
    w_spec = lambda shape: pl.BlockSpec(shape, lambda i, be, nu: (be[last_used(i, nu)], 0, 0))
    grid_spec = pltpu.PrefetchScalarGridSpec(
        num_scalar_prefetch=2,
        grid=(n_blocks,),
        in_specs=[
            pl.BlockSpec((block_lines, LANES), lambda i, be, nu: (last_used(i, nu), 0)),
            w_spec((1, d, D_EXPERT)), w_spec((1, d, D_EXPERT)), w_spec((1, D_EXPERT, d)),
        ],
        out_specs=pl.BlockSpec((block_lines, LANES), lambda i, be, nu: (i, 0)),
    )
    return pl.pallas_call(
        _expert_kernel,
        grid_spec=grid_spec,
        out_shape=jax.ShapeDtypeStruct(x_lines.shape, F32),
        compiler_params=pltpu.CompilerParams(dimension_semantics=("arbitrary",)),
        name="experts",
    )(block_expert, n_used, x_lines, w_gate.astype(BF16), w_up.astype(BF16), w_down.astype(BF16))


def _final_kernel(pos_ref, pos_next_ref, x1_ref, mod_ref, route_ref, ln_g_ref, ln_b_ref, y_hbm, o_ref,
                  ybuf, sem):
    i = pl.program_id(0)
    slot = i % 2
    tt = x1_ref.shape[1]

    def start_gather(idx_ref, to_slot):
        def group(g, carry):
            for j in range(DMA_UNROLL):
                t = g * DMA_UNROLL + j
                for k in range(TOP_K):
                    pltpu.make_async_copy(y_hbm.at[_lines(idx_ref[0, 0, t * TOP_K + k])],
                                          ybuf.at[to_slot, _lines(k * tt + t)],
                                          sem.at[to_slot]).start(priority=k)
            return carry
        lax.fori_loop(0, tt // DMA_UNROLL, group, 0)

    @pl.when(i == 0)
    def _():
        start_gather(pos_ref, 0)

    @pl.when(i < pl.num_programs(0) - 1)
    def _():
        start_gather(pos_next_ref, 1 - slot)

    pltpu.make_async_copy(y_hbm.at[pl.ds(0, ybuf.shape[1])], ybuf.at[slot], sem.at[slot]).wait()
    x1 = x1_ref[0]
    gate2 = mod_ref[0, 5:6, :]
    rt = route_ref[...]
    rows = ybuf.at[slot]
    y0 = _load_rows(rows.at[pl.ds(0, tt * ROW_LINES)], tt)
    y1 = _load_rows(rows.at[pl.ds(tt * ROW_LINES, tt * ROW_LINES)], tt)
    r = DEEP_ALPHA * x1 + gate2 * (rt[:, 2:3] * y0 + rt[:, 3:4] * y1)
    mu = jnp.mean(r, axis=-1, keepdims=True)
    rc = r - mu
    var = jnp.mean(rc * rc, axis=-1, keepdims=True)
    o_ref[0] = rc * lax.rsqrt(var + LN_EPS) * ln_g_ref[...] + ln_b_ref[...]


def _final(x1, mod, y_lines, route, pos_steps, ln_g, ln_b):
    n_b, seq, d = x1.shape
    tt = FINAL_TILE
    per_b = seq // tt
    n_steps = n_b * per_b
    pos_spec = lambda shift: pl.BlockSpec((1, 1, TOP_K * tt), lambda i: (jnp.minimum(i + shift, n_steps - 1), 0, 0),
                                          memory_space=pltpu.SMEM)
    return pl.pallas_call(
        _final_kernel,
        grid=(n_steps,),
        in_specs=[
            pos_spec(0), pos_spec(1),
            pl.BlockSpec((1, tt, d), lambda i: (i // per_b, i % per_b, 0)),
            pl.BlockSpec((1,) + mod.shape[1:], lambda i: (i // per_b, 0, 0)),
            pl.BlockSpec((tt, LANES), lambda i: (i, 0)),
            pl.BlockSpec((1, d), lambda i: (0, 0)),
            pl.BlockSpec((1, d), lambda i: (0, 0)),
            pl.BlockSpec(memory_space=pl.ANY),
        ],
        out_specs=pl.BlockSpec((1, tt, d), lambda i: (i // per_b, i % per_b, 0)),
        out_shape=jax.ShapeDtypeStruct((n_b, seq, d), F32),
        scratch_shapes=[pltpu.VMEM((2, TOP_K * tt * ROW_LINES, LANES), F32), pltpu.SemaphoreType.DMA((2,))],
        compiler_params=pltpu.CompilerParams(dimension_semantics=("arbitrary",), vmem_limit_bytes=VMEM_LIMIT),
        name="final",
    )(pos_steps, pos_steps, x1, mod, route, ln_g.reshape(1, d), ln_b.reshape(1, d), y_lines)


def kernel(x, c, w_ada, b_ada, w_in, conv_w, conv_norm_w, dn_conv_w, dn_A_log, dn_dt_bias, dn_norm_w,
           w_out, ln1_g, ln1_b, w_grp, b_grp, w_exp, b_exp, w_gate, w_up, w_down, ln2_g, ln2_b):
    n_b, seq, d = x.shape
    n_tok = n_b * seq
    for l in range(w_ada.shape[0]):
        mod = _ada(c, w_ada[l], b_ada[l].reshape(1, -1)).reshape(n_b, 6, d)
        x1, h2, logits = _mixer(x, mod, w_in[l], conv_w[l], conv_norm_w[l], dn_conv_w[l], dn_A_log[l],
                                dn_dt_bias[l], dn_norm_w[l], w_out[l], ln1_g[l], ln1_b[l],
                                w_grp[l], b_grp[l], w_exp[l], b_exp[l])
        route, counts = _route(logits.reshape(n_tok, LANES))
        pstart, block_expert, n_valid, n_used = _block_table(counts, n_tok * TOP_K)
        pos = _positions(route, pstart)[:, 0:TOP_K]
        pos_steps = pos.reshape(n_tok // FINAL_TILE, 1, TOP_K * FINAL_TILE)
        x_rows = _dispatch(h2.reshape(n_tok * ROW_LINES, LANES), pos_steps, n_valid)
        y_rows = _experts(x_rows, block_expert, n_used, w_gate[l], w_up[l], w_down[l])
        x = _final(x1, mod, y_rows, route, pos_steps, ln2_g[l], ln2_b[l])
    return x
```

```python
import functools

import numpy as np
import jax
import jax.numpy as jnp
from jax import lax
from jax.experimental import pallas as pl
from jax.experimental.pallas import tpu as pltpu

F32 = jnp.float32
BF16 = jnp.bfloat16

D_MODEL = 1024
D_CONV = 512
CONV_GROUP = 64
DN_HEADS = 4
DN_HEAD_DIM = 128
D_DELTA = DN_HEADS * DN_HEAD_DIM
N_GROUPS = 4
EXPERTS_PER_GROUP = 8
N_EXPERTS = N_GROUPS * EXPERTS_PER_GROUP
TOP_K = 2
D_EXPERT = 256
LN_EPS = 1e-5
RMS_EPS = 1e-6
DEEP_ALPHA = 2.0 ** 0.25

LANES = 128
MXU_TILE = 256
SUBLANES = 8
HALO = SUBLANES
SEQ_TILE = 512
CHUNK = 128
N_LEVELS = 7
POS_TILE = 1024
ROW_BLOCK = 512
ROW_LINES = D_MODEL // LANES
DMA_UNROLL = 8
FINAL_TILE = 512
VMEM_LIMIT = 56 * 1024 * 1024


def _dot(a, b):
    return jnp.dot(a, b, preferred_element_type=F32)


def _dot_nt(a, b):
    return lax.dot_general(a, b, (((1,), (1,)), ((), ())), preferred_element_type=F32)


def _split(x):
    hi = x.astype(BF16)
    lo = (x - hi.astype(F32)).astype(BF16)
    return hi, lo


def _dot_split_lhs(x, m):
    hi, lo = _split(x)
    return _dot(hi, m) + _dot(lo, m)


def _group_sums(x, g):
    w = g.shape[0]
    return jnp.concatenate([_dot_split_lhs(x[:, i:i + w], g) for i in range(0, x.shape[1], w)], axis=1)


def _head_sums(x):
    w = DN_HEAD_DIM
    return jnp.concatenate([jnp.broadcast_to(jnp.sum(x[:, c:c + w], axis=-1, keepdims=True), (x.shape[0], w))
                            for c in range(0, x.shape[1], w)], axis=1)


def _head_bcast(cols, first):
    return jnp.concatenate([jnp.broadcast_to(cols[:, first + h:first + h + 1], (cols.shape[0], DN_HEAD_DIM))
                            for h in range(DN_HEADS)], axis=1)


def _dot_split_rhs(m, x):
    hi, lo = _split(x)
    return _dot(m, hi) + _dot(m, lo)


def _sigmoid(x):
    return 1.0 / (1.0 + jnp.exp(-x))


def _silu(x):
    return x * _sigmoid(x)


def _load_rows(ref, n_rows):
    return jnp.concatenate([ref[pl.ds(s, n_rows, stride=ROW_LINES), :] for s in range(ROW_LINES)], axis=1)


def _store_rows(ref, val):
    for s in range(ROW_LINES):
        ref[pl.ds(s, val.shape[0], stride=ROW_LINES), :] = val[:, s * LANES:(s + 1) * LANES]


def _lines(row, n=1):
    return pl.ds(pl.multiple_of(row * ROW_LINES, ROW_LINES), n * ROW_LINES)


def _ada_kernel(c_ref, w_ref, b_ref, o_ref):
    c = c_ref[...]
    o_ref[...] = _dot(_silu(c).astype(BF16), w_ref[...].astype(BF16)) + b_ref[...]


def _ada(c, w, b):
    n_b, d = c.shape
    n_out = w.shape[1]
    return pl.pallas_call(
        _ada_kernel,
        grid=(n_out // d,),
        in_specs=[
            pl.BlockSpec((n_b, d), lambda i: (0, 0)),
            pl.BlockSpec((d, d), lambda i: (0, i)),
            pl.BlockSpec((1, d), lambda i: (0, i)),
        ],
        out_specs=pl.BlockSpec((n_b, d), lambda i: (0, i)),
        out_shape=jax.ShapeDtypeStruct((n_b, n_out), F32),
        name="ada",
    )(c, w, b)


def _shifted(full, k):
    return pltpu.roll(full, k, axis=0)[HALO:]


def _route_tile(lg):
    lane = lax.broadcasted_iota(jnp.int32, lg.shape, 1).astype(F32)
    neg = -jnp.inf
    big = float(LANES)
    is_grp = lane < N_GROUPS
    gl = jnp.where(is_grp, lg, neg)
    gm = jnp.max(gl, axis=-1, keepdims=True)
    gi = jnp.min(jnp.where(gl == gm, lane, big), axis=-1, keepdims=True)
    grp_w = 1.0 / jnp.sum(jnp.where(is_grp, jnp.exp(lg - gm), 0.0), axis=-1, keepdims=True)
    lo = N_GROUPS + gi * EXPERTS_PER_GROUP
    el = jnp.where((lane >= lo) & (lane < lo + EXPERTS_PER_GROUP), lg, neg)
    m1 = jnp.max(el, axis=-1, keepdims=True)
    i1 = jnp.min(jnp.where(el == m1, lane, big), axis=-1, keepdims=True)
    el2 = jnp.where(lane == i1, neg, el)
    m2 = jnp.max(el2, axis=-1, keepdims=True)
    i2 = jnp.min(jnp.where(el2 == m2, lane, big), axis=-1, keepdims=True)
    ratio = jnp.exp(m2 - m1)
    g0 = grp_w / (1.0 + ratio)
    g1 = g0 * ratio
    e0 = i1 - N_GROUPS
    e1 = i2 - N_GROUPS
    route = jnp.where(lane == 0, e0, jnp.where(lane == 1, e1, jnp.where(lane == 2, g0,
                                                                       jnp.where(lane == 3, g1, 0.0))))
    chosen = ((lane == e0) | (lane == e1)).astype(F32)
    return route, jnp.sum(chosen, axis=0, keepdims=True)


def _mixer_kernel(x_ref, mod_ref, w_main_ref, w_ba_ref, conv_w_ref, conv_nw_ref, dn_conv_w_ref,
                  a_row_ref, dt_row_ref, dn_nw_ref, w_out_ref, ln_g_ref, ln_b_ref, w_rt_ref, b_rt_ref,
                  g64_ref, tri_ref, lvl_ref,
                  x1_ref, h2_ref, route_ref, cnt_ref,
                  cu_buf, qkv_buf, s_ref, q_s, k_s, v_s, o_s, beta_s, gc_s, u_s,
                  w_s, qkg_s, qd_s, kend_t_s):
    ts = x_ref.shape[1]

    @pl.when(pl.program_id(1) == 0)
    def _():
        cu_buf[0:HALO, :] = jnp.zeros((HALO, D_CONV), F32)
        qkv_buf[0:HALO, :] = jnp.zeros((HALO, 3 * D_DELTA), F32)
        s_ref[...] = jnp.zeros(s_ref.shape, F32)

    x = x_ref[0]
    shift1 = mod_ref[0, 0:1, :]
    scale1 = mod_ref[0, 1:2, :]
    gate1 = mod_ref[0, 2:3, :]
    shift2 = mod_ref[0, 3:4, :]
    scale2 = mod_ref[0, 4:5, :]
    h = (x * (1.0 + scale1) + shift1).astype(BF16)

    b_gate = _dot(h, w_main_ref[:, 0:D_CONV])
    c_gate = _dot(h, w_main_ref[:, D_CONV:2 * D_CONV])
    u_in = _dot(h, w_main_ref[:, 2 * D_CONV:3 * D_CONV])
    cu = c_gate * u_in
    cu_buf[HALO:HALO + ts, :] = cu
    full = cu_buf[...]
    cw = conv_w_ref[...]
    conv = cw[0:1, :] * _shifted(full, 2) + cw[1:2, :] * _shifted(full, 1) + cw[2:3, :] * cu
    cu_buf[0:HALO, :] = cu[ts - HALO:ts, :]
    y = b_gate * conv
    ms = _group_sums(y * y, g64_ref[...]) * (1.0 / CONV_GROUP)
    y_conv = y * lax.rsqrt(ms + RMS_EPS) * conv_nw_ref[...]

    off = 3 * D_CONV
    qkv = _dot(h, w_main_ref[:, off:off + 3 * D_DELTA])
    qkv_buf[HALO:HALO + ts, :] = qkv
    fullq = qkv_buf[...]
    w4 = dn_conv_w_ref[...]
    acc = (w4[0:1, :] * _shifted(fullq, 3) + w4[1:2, :] * _shifted(fullq, 2)
           + w4[2:3, :] * _shifted(fullq, 1) + w4[3:4, :] * qkv)
    qkv_buf[0:HALO, :] = qkv[ts - HALO:ts, :]
    act = _silu(acc)
    q = act[:, 0:D_DELTA]
    k = act[:, D_DELTA:2 * D_DELTA]
    q_s[...] = q * lax.rsqrt(_head_sums(q * q) + RMS_EPS) * (DN_HEAD_DIM ** -0.5)
    k_s[...] = k * lax.rsqrt(_head_sums(k * k) + RMS_EPS)
    v_s[...] = act[:, 2 * D_DELTA:3 * D_DELTA]

    ba = _dot(h, w_ba_ref[...])
    beta_all = _sigmoid(ba)
    a_in = ba + dt_row_ref[...]
    softplus = jnp.maximum(a_in, 0.0) + jnp.log1p(jnp.exp(-jnp.abs(a_in)))
    g_all = -jnp.exp(a_row_ref[...]) * softplus
    gc_all = _dot_split_rhs(tri_ref[...], g_all)
    beta_s[...] = _head_bcast(beta_all, 0)
    gc_s[...] = _head_bcast(gc_all, DN_HEADS)

    row = lax.broadcasted_iota(jnp.int32, (CHUNK, CHUNK), 0)
    col = lax.broadcasted_iota(jnp.int32, (CHUNK, CHUNK), 1)
    incl = row >= col
    eye = (row == col).astype(F32)
    blocks = [(slice(c * CHUNK, (c + 1) * CHUNK), slice(hd * DN_HEAD_DIM, (hd + 1) * DN_HEAD_DIM))
              for c in range(ts // CHUNK) for hd in range(DN_HEADS)]

    lmats, t_invs = [], []
    for rows, cols in blocks:
        qh = q_s[rows, cols]
        kh = k_s[rows, cols]
        gc = gc_s[rows, cols]
        gam = jnp.exp(jnp.where(incl, gc - gc.T, -jnp.inf))
        kb = kh.astype(BF16)
        lmat = beta_s[rows, cols] * _dot_nt(kb, kb) * gam
        qkg_s[rows, cols] = (_dot_nt(qh.astype(BF16), kb) * gam).astype(BF16)
        eg = jnp.exp(gc)
        qd_s[rows, cols] = (qh * eg).astype(BF16)
        k_end = kh * jnp.exp(gc[CHUNK - 1:CHUNK, :] - gc)
        kend_t_s[rows, cols] = k_end.T.astype(BF16)
        lmats.append(lmat)
        t_invs.append(eye - lmat * lvl_ref[0])
    for lv in range(1, N_LEVELS):
        tbs = [t.astype(BF16) for t in t_invs]
        ys = [_dot((lm * lvl_ref[lv]).astype(BF16), tb).astype(BF16) for lm, tb in zip(lmats, tbs)]
        t_invs = [t - _dot(tb, y) for t, tb, y in zip(t_invs, tbs, ys)]
    for (rows, cols), t_inv in zip(blocks, t_invs):
        beta = beta_s[rows, cols]
        kh = k_s[rows, cols]
        rhs = jnp.concatenate([beta * v_s[rows, cols], beta * jnp.exp(gc_s[rows, cols]) * kh], axis=1)
        sol = _dot(t_inv.astype(BF16), rhs.astype(BF16))
        u_s[rows, cols] = sol[:, 0:DN_HEAD_DIM]
        w_s[rows, cols] = sol[:, DN_HEAD_DIM:2 * DN_HEAD_DIM].astype(BF16)

    heads = [slice(hd * DN_HEAD_DIM, (hd + 1) * DN_HEAD_DIM) for hd in range(DN_HEADS)]
    for c in range(ts // CHUNK):
        rows = slice(c * CHUNK, (c + 1) * CHUNK)
        states = [s_ref[hd] for hd in range(DN_HEADS)]
        sbs = [st.astype(BF16) for st in states]
        dbs = [(u_s[rows, cols] - _dot(w_s[rows, cols], sb)).astype(BF16) for cols, sb in zip(heads, sbs)]
        for hd, cols in enumerate(heads):
            o_s[rows, cols] = _dot(qd_s[rows, cols], sbs[hd]) + _dot(qkg_s[rows, cols], dbs[hd])
            decay = jnp.exp(gc_s[(c + 1) * CHUNK - 1:(c + 1) * CHUNK, cols])
            s_ref[hd] = states[hd] * decay + _dot(kend_t_s[rows, cols], dbs[hd])

    o = o_s[...]
    z = _dot(h, w_main_ref[:, off + 3 * D_DELTA:off + 4 * D_DELTA])
    o_ms = _head_sums(o * o) * (1.0 / DN_HEAD_DIM)
    y_dn = o * lax.rsqrt(o_ms + RMS_EPS) * dn_nw_ref[...] * _silu(z)
    mix_in = jnp.concatenate([y_conv, y_dn], axis=1).astype(BF16)
    mix = _dot(mix_in, w_out_ref[...])
    r = DEEP_ALPHA * x + gate1 * mix
    mu = jnp.mean(r, axis=-1, keepdims=True)
    rc = r - mu
    var = jnp.mean(rc * rc, axis=-1, keepdims=True)
    x1 = rc * lax.rsqrt(var + LN_EPS) * ln_g_ref[...] + ln_b_ref[...]
    x1_ref[0] = x1
    h2 = x1 * (1.0 + scale2) + shift2
    _store_rows(h2_ref.at[0], h2)
    route, counts = _route_tile(_dot(h2.astype(BF16), w_rt_ref[...]) + b_rt_ref[...])
    route_ref[0] = route
    cnt_ref[0] = jnp.broadcast_to(counts, cnt_ref.shape[1:])


def _const_spec(shape):
    nd = len(shape)
    return pl.BlockSpec(shape, lambda b, j, _nd=nd: (0,) * _nd, pipeline_mode=pl.Buffered(1))


def _mixer_constants(ts):
    m = np.arange(MXU_TILE)
    g64 = (m[:, None] // CONV_GROUP == m[None, :] // CONV_GROUP).astype(np.float32)
    t = np.arange(ts)
    tri = ((t[:, None] // CHUNK == t[None, :] // CHUNK) & (t[None, :] <= t[:, None])).astype(np.float32)
    r = np.arange(CHUNK)
    lvl = np.stack([(((r[:, None] >> l) ^ (r[None, :] >> l)) == 1) & (r[:, None] > r[None, :])
                    for l in range(N_LEVELS)]).astype(np.float32)
    as_bf16 = lambda a: jnp.asarray(a, dtype=BF16)
    return as_bf16(g64), as_bf16(tri), jnp.asarray(lvl)


def _mixer(x, mod, w_in, conv_w, conv_norm_w, dn_conv_w, dn_a_log, dn_dt_bias, dn_norm_w, w_out,
           ln_g, ln_b, w_grp, b_grp, w_exp, b_exp):
    n_b, seq, d = x.shape
    ts = SEQ_TILE
    n_main = 3 * D_CONV + 4 * D_DELTA
    w_main = w_in[:, :n_main].astype(BF16)
    w_ba = jnp.pad(w_in[:, n_main:], ((0, 0), (0, LANES - 2 * DN_HEADS))).astype(BF16)
    head_pad = (DN_HEADS, LANES - 2 * DN_HEADS)
    a_row = jnp.pad(dn_a_log, head_pad).reshape(1, LANES)
    dt_row = jnp.pad(dn_dt_bias, head_pad).reshape(1, LANES)
    n_rt = N_GROUPS + N_EXPERTS
    w_rt = jnp.pad(jnp.concatenate([w_grp, w_exp], axis=1), ((0, 0), (0, LANES - n_rt))).astype(BF16)
    b_rt = jnp.pad(jnp.concatenate([b_grp, b_exp]), (0, LANES - n_rt)).reshape(1, LANES)
    consts = _mixer_constants(ts)
    operands = (
        x, mod, w_main, w_ba, conv_w, conv_norm_w.reshape(1, D_CONV), dn_conv_w, a_row, dt_row,
        jnp.tile(dn_norm_w, DN_HEADS).reshape(1, D_DELTA), w_out.astype(BF16),
        ln_g.reshape(1, d), ln_b.reshape(1, d), w_rt, b_rt) + consts
    in_specs = [
        pl.BlockSpec((1, ts, d), lambda b, j: (b, j, 0)),
        pl.BlockSpec((1,) + mod.shape[1:], lambda b, j: (b, 0, 0)),
    ] + [_const_spec(a.shape) for a in operands[2:]]
    tile_spec = lambda width: pl.BlockSpec((1, ts, width), lambda b, j: (b, j, 0))
    return pl.pallas_call(
        _mixer_kernel,
        grid=(n_b, seq // ts),
        in_specs=in_specs,
        out_specs=[tile_spec(d), pl.BlockSpec((1, ts * ROW_LINES, LANES), lambda b, j: (b, j, 0)),
                   tile_spec(LANES), pl.BlockSpec((1, SUBLANES, LANES), lambda b, j: (b, j, 0))],
        out_shape=[jax.ShapeDtypeStruct((n_b, seq, d), F32),
                   jax.ShapeDtypeStruct((n_b, seq * ROW_LINES, LANES), F32),
                   jax.ShapeDtypeStruct((n_b, seq, LANES), F32),
                   jax.ShapeDtypeStruct((n_b, seq // ts * SUBLANES, LANES), F32)],
        scratch_shapes=[
            pltpu.VMEM((HALO + ts, D_CONV), F32),
            pltpu.VMEM((HALO + ts, 3 * D_DELTA), F32),
            pltpu.VMEM((DN_HEADS, DN_HEAD_DIM, DN_HEAD_DIM), F32),
        ] + [pltpu.VMEM((ts, D_DELTA), F32)] * 7 + [pltpu.VMEM((ts, D_DELTA), BF16)] * 4,
        compiler_params=pltpu.CompilerParams(dimension_semantics=("arbitrary", "arbitrary"),
                                             vmem_limit_bytes=VMEM_LIMIT),
        name="mixer",
    )(*operands)


def _block_table(counts, n_asg):
    padded = ((counts + ROW_BLOCK - 1) // ROW_BLOCK) * ROW_BLOCK
    pend = jnp.cumsum(padded)
    pstart = pend - padded
    n_blocks = n_asg // ROW_BLOCK + N_EXPERTS
    block_start = jnp.arange(n_blocks, dtype=jnp.int32) * ROW_BLOCK
    block_expert = jnp.minimum(jnp.sum(pend[None, :] <= block_start[:, None], axis=1),
                               N_EXPERTS - 1).astype(jnp.int32)
    n_valid = jnp.clip((pstart + counts)[block_expert] - block_start, 0, ROW_BLOCK).astype(jnp.int32)
    n_used = (pend[-1:] // ROW_BLOCK).astype(jnp.int32)
    return pstart, block_expert, n_valid, n_used


def _pos_kernel(route_ref, pstart_ref, tri_ref, pos_ref, running):
    @pl.when(pl.program_id(0) == 0)
    def _():
        running[...] = jnp.zeros(running.shape, F32)

    rt = route_ref[...]
    lane = lax.broadcasted_iota(jnp.int32, rt.shape, 1).astype(F32)
    oh0 = (lane == rt[:, 0:1]).astype(F32)
    oh1 = (lane == rt[:, 1:2]).astype(F32)
    both = oh0 + oh1
    before = pstart_ref[...] + running[0:1, :] + _dot(tri_ref[...], both.astype(BF16))
    pos0 = jnp.sum(oh0 * before, axis=-1, keepdims=True)
    pos1 = jnp.sum(oh1 * before, axis=-1, keepdims=True)
    pos_ref[...] = jnp.where(lane == 0, pos0, jnp.where(lane == 1, pos1, 0.0)).astype(jnp.int32)
    running[...] = running[...] + jnp.sum(both, axis=0, keepdims=True)


def _positions(route, pstart):
    n_tok = route.shape[0]
    t = np.arange(POS_TILE)
    tri = jnp.asarray((t[None, :] < t[:, None]).astype(np.float32), dtype=BF16)
    pstart_row = jnp.pad(pstart.astype(F32), (0, LANES - N_EXPERTS)).reshape(1, LANES)
    spec = pl.BlockSpec((POS_TILE, LANES), lambda i: (i, 0))
    return pl.pallas_call(
        _pos_kernel,
        grid=(n_tok // POS_TILE,),
        in_specs=[spec, pl.BlockSpec((1, LANES), lambda i: (0, 0)),
                  pl.BlockSpec((POS_TILE, POS_TILE), lambda i: (0, 0), pipeline_mode=pl.Buffered(1))],
        out_specs=spec,
        out_shape=jax.ShapeDtypeStruct((n_tok, LANES), jnp.int32),
        scratch_shapes=[pltpu.VMEM((SUBLANES, LANES), F32)],
        compiler_params=pltpu.CompilerParams(dimension_semantics=("arbitrary",)),
        name="positions",
    )(route, pstart_row, tri)


def _dispatch_kernel(nv_ref, pos_ref, h2_ref, x_hbm, zbuf, sem_fill, sem_rows):
    i = pl.program_id(0)
    n_blocks = nv_ref.shape[0]
    n_tok_step = pos_ref.shape[2] // TOP_K

    def fill(b):
        return pltpu.make_async_copy(zbuf, x_hbm.at[_lines(b * ROW_BLOCK, ROW_BLOCK)], sem_fill)

    @pl.when(i == 0)
    def _():
        zbuf[...] = jnp.zeros(zbuf.shape, F32)

        def start_fill(b, carry):
            @pl.when(nv_ref[b] < ROW_BLOCK)
            def _():
                fill(b).start()
            return carry

        def wait_fill(b, carry):
            @pl.when(nv_ref[b] < ROW_BLOCK)
            def _():
                fill(b).wait()
            return carry

        lax.fori_loop(0, n_blocks, start_fill, 0)
        lax.fori_loop(0, n_blocks, wait_fill, 0)

    def group(g, carry):
        for j in range(DMA_UNROLL):
            t = g * DMA_UNROLL + j
            src = h2_ref.at[_lines(t)]
            for k in range(TOP_K):
                pltpu.make_async_copy(src, x_hbm.at[_lines(pos_ref[0, 0, t * TOP_K + k])],
                                      sem_rows).start(priority=k)
        return carry

    lax.fori_loop(0, n_tok_step // DMA_UNROLL, group, 0)
    for _ in range(TOP_K):
        pltpu.make_async_copy(h2_ref, x_hbm.at[pl.ds(0, h2_ref.shape[0])], sem_rows).wait()


def _dispatch(h2_lines, pos_steps, n_valid):
    n_blocks = n_valid.shape[0]
    step_lines = pos_steps.shape[2] // TOP_K * ROW_LINES
    grid_spec = pltpu.PrefetchScalarGridSpec(
        num_scalar_prefetch=1,
        grid=(pos_steps.shape[0],),
        in_specs=[pl.BlockSpec((1, 1, pos_steps.shape[2]), lambda i, nv: (i, 0, 0), memory_space=pltpu.SMEM),
                  pl.BlockSpec((step_lines, LANES), lambda i, nv: (i, 0))],
        out_specs=pl.BlockSpec(memory_space=pl.ANY),
        scratch_shapes=[pltpu.VMEM((ROW_BLOCK * ROW_LINES, LANES), F32),
                        pltpu.SemaphoreType.DMA(()), pltpu.SemaphoreType.DMA(())],
    )
    return pl.pallas_call(
        _dispatch_kernel,
        grid_spec=grid_spec,
        out_shape=jax.ShapeDtypeStruct((n_blocks * ROW_BLOCK * ROW_LINES, LANES), F32),
        compiler_params=pltpu.CompilerParams(dimension_semantics=("arbitrary",)),
        name="dispatch",
    )(n_valid, pos_steps, h2_lines)


def _expert_kernel(be_ref, nu_ref, x_ref, wg_ref, wu_ref, wd_ref, y_ref):
    del be_ref
    used = pl.program_id(0) < nu_ref[0]

    @pl.when(used)
    def _():
        xb = _load_rows(x_ref, ROW_BLOCK).astype(BF16)
        gate = _dot(xb, wg_ref[0])
        up = _dot(xb, wu_ref[0])
        _store_rows(y_ref, _dot((_silu(gate) * up).astype(BF16), wd_ref[0]))

    @pl.when(jnp.logical_not(used))
    def _():
        y_ref[...] = jnp.zeros(y_ref.shape, F32)


def _experts(x_lines, block_expert, n_used, w_gate, w_up, w_down):
    n_blocks = block_expert.shape[0]
    d = w_gate.shape[1]
    block_lines = ROW_BLOCK * ROW_LINES
    last_used = lambda i, nu: jnp.minimum(i, jnp.maximum(nu[0] - 1, 0))
    w_spec = lambda shape: pl.BlockSpec(shape, lambda i, be, nu: (be[last_used(i, nu)], 0, 0))
    grid_spec = pltpu.PrefetchScalarGridSpec(
        num_scalar_prefetch=2,
        grid=(n_blocks,),
        in_specs=[
            pl.BlockSpec((block_lines, LANES), lambda i, be, nu: (last_used(i, nu), 0)),
            w_spec((1, d, D_EXPERT)), w_spec((1, d, D_EXPERT)), w_spec((1, D_EXPERT, d)),
        ],
        out_specs=pl.BlockSpec((block_lines, LANES), lambda i, be, nu: (i, 0)),
    )
    return pl.pallas_call(
        _expert_kernel,
        grid_spec=grid_spec,
        out_shape=jax.ShapeDtypeStruct(x_lines.shape, F32),
        compiler_params=pltpu.CompilerParams(dimension_semantics=("arbitrary",)),
        name="experts",
    )(block_expert, n_used, x_lines, w_gate.astype(BF16), w_up.astype(BF16), w_down.astype(BF16))


def _final_kernel(pos_ref, pos_next_ref, x1_ref, mod_ref, route_ref, ln_g_ref, ln_b_ref, y_hbm, o_ref,
                  ybuf, sem):
    i = pl.program_id(0)
    slot = i % 2
    tt = x1_ref.shape[1]

    def start_gather(idx_ref, to_slot):
        def group(g, carry):
            for j in range(DMA_UNROLL):
                t = g * DMA_UNROLL + j
                for k in range(TOP_K):
                    pltpu.make_async_copy(y_hbm.at[_lines(idx_ref[0, 0, t * TOP_K + k])],
                                          ybuf.at[to_slot, _lines(k * tt + t)],
                                          sem.at[to_slot]).start(priority=k)
            return carry
        lax.fori_loop(0, tt // DMA_UNROLL, group, 0)

    @pl.when(i == 0)
    def _():
        start_gather(pos_ref, 0)

    @pl.when(i < pl.num_programs(0) - 1)
    def _():
        start_gather(pos_next_ref, 1 - slot)

    pltpu.make_async_copy(y_hbm.at[pl.ds(0, ybuf.shape[1])], ybuf.at[slot], sem.at[slot]).wait()
    x1 = x1_ref[0]
    gate2 = mod_ref[0, 5:6, :]
    rt = route_ref[...]
    rows = ybuf.at[slot]
    y0 = _load_rows(rows.at[pl.ds(0, tt * ROW_LINES)], tt)
    y1 = _load_rows(rows.at[pl.ds(tt * ROW_LINES, tt * ROW_LINES)], tt)
    r = DEEP_ALPHA * x1 + gate2 * (rt[:, 2:3] * y0 + rt[:, 3:4] * y1)
    mu = jnp.mean(r, axis=-1, keepdims=True)
    rc = r - mu
    var = jnp.mean(rc * rc, axis=-1, keepdims=True)
    o_ref[0] = rc * lax.rsqrt(var + LN_EPS) * ln_g_ref[...] + ln_b_ref[...]


def _final(x1, mod, y_lines, route, pos_steps, ln_g, ln_b):
    n_b, seq, d = x1.shape
    tt = FINAL_TILE
    per_b = seq // tt
    n_steps = n_b * per_b
    pos_spec = lambda shift: pl.BlockSpec((1, 1, TOP_K * tt), lambda i: (jnp.minimum(i + shift, n_steps - 1), 0, 0),
                                          memory_space=pltpu.SMEM)
    return pl.pallas_call(
        _final_kernel,
        grid=(n_steps,),
        in_specs=[
            pos_spec(0), pos_spec(1),
            pl.BlockSpec((1, tt, d), lambda i: (i // per_b, i % per_b, 0)),
            pl.BlockSpec((1,) + mod.shape[1:], lambda i: (i // per_b, 0, 0)),
            pl.BlockSpec((tt, LANES), lambda i: (i, 0)),
            pl.BlockSpec((1, d), lambda i: (0, 0)),
            pl.BlockSpec((1, d), lambda i: (0, 0)),
            pl.BlockSpec(memory_space=pl.ANY),
        ],
        out_specs=pl.BlockSpec((1, tt, d), lambda i: (i // per_b, i % per_b, 0)),
        out_shape=jax.ShapeDtypeStruct((n_b, seq, d), F32),
        scratch_shapes=[pltpu.VMEM((2, TOP_K * tt * ROW_LINES, LANES), F32), pltpu.SemaphoreType.DMA((2,))],
        compiler_params=pltpu.CompilerParams(dimension_semantics=("arbitrary",), vmem_limit_bytes=VMEM_LIMIT),
        name="final",
    )(pos_steps, pos_steps, x1, mod, route, ln_g.reshape(1, d), ln_b.reshape(1, d), y_lines)


def kernel(x, c, w_ada, b_ada, w_in, conv_w, conv_norm_w, dn_conv_w, dn_A_log, dn_dt_bias, dn_norm_w,
           w_out, ln1_g, ln1_b, w_grp, b_grp, w_exp, b_exp, w_gate, w_up, w_down, ln2_g, ln2_b):
    n_b, seq, d = x.shape
    n_tok = n_b * seq
    for l in range(w_ada.shape[0]):
        mod = _ada(c, w_ada[l], b_ada[l].reshape(1, -1)).reshape(n_b, 6, d)
        x1, h2, route, tile_counts = _mixer(x, mod, w_in[l], conv_w[l], conv_norm_w[l], dn_conv_w[l],
                                            dn_A_log[l], dn_dt_bias[l], dn_norm_w[l], w_out[l], ln1_g[l],
                                            ln1_b[l], w_grp[l], b_grp[l], w_exp[l], b_exp[l])
        route = route.reshape(n_tok, LANES)
        counts = jnp.sum(tile_counts.reshape(-1, SUBLANES, LANES)[:, 0, :N_EXPERTS], axis=0).astype(jnp.int32)
        pstart, block_expert, n_valid, n_used = _block_table(counts, n_tok * TOP_K)
        pos = _positions(route, pstart)[:, 0:TOP_K]
        pos_steps = pos.reshape(n_tok // FINAL_TILE, 1, TOP_K * FINAL_TILE)
        x_rows = _dispatch(h2.reshape(n_tok * ROW_LINES, LANES), pos_steps, n_valid)
        y_rows = _experts(x_rows, block_expert, n_used, w_gate[l], w_up[l], w_down[l])
        x = _final(x1, mod, y_rows, route, pos_steps, ln2_g[l], ln2_b[l])
    return x
```

```python
import functools

import numpy as np
import jax
import jax.numpy as jnp
from jax import lax
from jax.experimental import pallas as pl
from jax.experimental.pallas import tpu as pltpu

F32 = jnp.float32
BF16 = jnp.bfloat16

D_MODEL = 1024
D_CONV = 512
CONV_GROUP = 64
DN_HEADS = 4
DN_HEAD_DIM = 128
D_DELTA = DN_HEADS * DN_HEAD_DIM
N_GROUPS = 4
EXPERTS_PER_GROUP = 8
N_EXPERTS = N_GROUPS * EXPERTS_PER_GROUP
TOP_K = 2
D_EXPERT = 256
LN_EPS = 1e-5
RMS_EPS = 1e-6
DEEP_ALPHA = 2.0 ** 0.25

LANES = 128
MXU_TILE = 256
SUBLANES = 8
HALO = SUBLANES
SEQ_TILE = 512
CHUNK = 128
N_LEVELS = 7
POS_TILE = 1024
ROW_BLOCK = 512
ROW_LINES = D_MODEL // LANES
DMA_UNROLL = 8
DISPATCH_TILE = 512
FINAL_TILE = 512
FINAL_CHUNK = 64
VMEM_LIMIT = 56 * 1024 * 1024


def _dot(a, b):
    return jnp.dot(a, b, preferred_element_type=F32)


def _dot_nt(a, b):
    return lax.dot_general(a, b, (((1,), (1,)), ((), ())), preferred_element_type=F32)


def _split(x):
    hi = x.astype(BF16)
    lo = (x - hi.astype(F32)).astype(BF16)
    return hi, lo


def _dot_split_lhs(x, m):
    hi, lo = _split(x)
    return _dot(hi, m) + _dot(lo, m)


def _group_sums(x, g):
    w = g.shape[0]
    return jnp.concatenate([_dot_split_lhs(x[:, i:i + w], g) for i in range(0, x.shape[1], w)], axis=1)


def _head_sums(x):
    w = DN_HEAD_DIM
    return jnp.concatenate([jnp.broadcast_to(jnp.sum(x[:, c:c + w], axis=-1, keepdims=True), (x.shape[0], w))
                            for c in range(0, x.shape[1], w)], axis=1)


def _head_bcast(cols, first):
    return jnp.concatenate([jnp.broadcast_to(cols[:, first + h:first + h + 1], (cols.shape[0], DN_HEAD_DIM))
                            for h in range(DN_HEADS)], axis=1)


def _dot_split_rhs(m, x):
    hi, lo = _split(x)
    return _dot(m, hi) + _dot(m, lo)


def _sigmoid(x):
    return 1.0 / (1.0 + jnp.exp(-x))


def _silu(x):
    return x * _sigmoid(x)


def _load_rows(ref, n_rows):
    return jnp.concatenate([ref[pl.ds(s, n_rows, stride=ROW_LINES), :] for s in range(ROW_LINES)], axis=1)


def _store_rows(ref, val):
    for s in range(ROW_LINES):
        ref[pl.ds(s, val.shape[0], stride=ROW_LINES), :] = val[:, s * LANES:(s + 1) * LANES]


def _lines(row, n=1):
    return pl.ds(pl.multiple_of(row * ROW_LINES, ROW_LINES), n * ROW_LINES)


def _ada_kernel(c_ref, w_ref, b_ref, o_ref):
    c = c_ref[...]
    o_ref[...] = _dot(_silu(c).astype(BF16), w_ref[...].astype(BF16)) + b_ref[...]


def _ada(c, w, b):
    n_b, d = c.shape
    n_out = w.shape[1]
    return pl.pallas_call(
        _ada_kernel,
        grid=(n_out // d,),
        in_specs=[
            pl.BlockSpec((n_b, d), lambda i: (0, 0)),
            pl.BlockSpec((d, d), lambda i: (0, i)),
            pl.BlockSpec((1, d), lambda i: (0, i)),
        ],
        out_specs=pl.BlockSpec((n_b, d), lambda i: (0, i)),
        out_shape=jax.ShapeDtypeStruct((n_b, n_out), F32),
        name="ada",
    )(c, w, b)


def _shifted(full, k):
    return pltpu.roll(full, k, axis=0)[HALO:]


def _route_tile(lg):
    lane = lax.broadcasted_iota(jnp.int32, lg.shape, 1).astype(F32)
    neg = -jnp.inf
    big = float(LANES)
    is_grp = lane < N_GROUPS
    gl = jnp.where(is_grp, lg, neg)
    gm = jnp.max(gl, axis=-1, keepdims=True)
    gi = jnp.min(jnp.where(gl == gm, lane, big), axis=-1, keepdims=True)
    grp_w = 1.0 / jnp.sum(jnp.where(is_grp, jnp.exp(lg - gm), 0.0), axis=-1, keepdims=True)
    lo = N_GROUPS + gi * EXPERTS_PER_GROUP
    el = jnp.where((lane >= lo) & (lane < lo + EXPERTS_PER_GROUP), lg, neg)
    m1 = jnp.max(el, axis=-1, keepdims=True)
    i1 = jnp.min(jnp.where(el == m1, lane, big), axis=-1, keepdims=True)
    el2 = jnp.where(lane == i1, neg, el)
    m2 = jnp.max(el2, axis=-1, keepdims=True)
    i2 = jnp.min(jnp.where(el2 == m2, lane, big), axis=-1, keepdims=True)
    ratio = jnp.exp(m2 - m1)
    g0 = grp_w / (1.0 + ratio)
    g1 = g0 * ratio
    e0 = i1 - N_GROUPS
    e1 = i2 - N_GROUPS
    route = jnp.where(lane == 0, e0, jnp.where(lane == 1, e1, jnp.where(lane == 2, g0,
                                                                       jnp.where(lane == 3, g1, 0.0))))
    chosen = ((lane == e0) | (lane == e1)).astype(F32)
    return route, jnp.sum(chosen, axis=0, keepdims=True)


def _mixer_kernel(x_ref, mod_ref, w_main_ref, w_ba_ref, conv_w_ref, conv_nw_ref, dn_conv_w_ref,
                  a_row_ref, dt_row_ref, dn_nw_ref, w_out_ref, ln_g_ref, ln_b_ref, w_rt_ref, b_rt_ref,
                  g64_ref, tri_ref, lvl_ref,
                  x1_ref, h2_ref, route_ref, cnt_ref,
                  cu_buf, qkv_buf, s_ref, q_s, k_s, v_s, o_s, beta_s, gc_s, u_s,
                  w_s, qkg_s, qd_s, kend_t_s):
    ts = x_ref.shape[1]

    @pl.when(pl.program_id(1) == 0)
    def _():
        cu_buf[0:HALO, :] = jnp.zeros((HALO, D_CONV), F32)
        qkv_buf[0:HALO, :] = jnp.zeros((HALO, 3 * D_DELTA), F32)
        s_ref[...] = jnp.zeros(s_ref.shape, F32)

    x = x_ref[0]
    shift1 = mod_ref[0, 0:1, :]
    scale1 = mod_ref[0, 1:2, :]
    gate1 = mod_ref[0, 2:3, :]
    shift2 = mod_ref[0, 3:4, :]
    scale2 = mod_ref[0, 4:5, :]
    h = (x * (1.0 + scale1) + shift1).astype(BF16)

    b_gate = _dot(h, w_main_ref[:, 0:D_CONV])
    c_gate = _dot(h, w_main_ref[:, D_CONV:2 * D_CONV])
    u_in = _dot(h, w_main_ref[:, 2 * D_CONV:3 * D_CONV])
    cu = c_gate * u_in
    cu_buf[HALO:HALO + ts, :] = cu
    full = cu_buf[...]
    cw = conv_w_ref[...]
    conv = cw[0:1, :] * _shifted(full, 2) + cw[1:2, :] * _shifted(full, 1) + cw[2:3, :] * cu
    cu_buf[0:HALO, :] = cu[ts - HALO:ts, :]
    y = b_gate * conv
    ms = _group_sums(y * y, g64_ref[...]) * (1.0 / CONV_GROUP)
    y_conv = y * lax.rsqrt(ms + RMS_EPS) * conv_nw_ref[...]

    off = 3 * D_CONV
    qkv = _dot(h, w_main_ref[:, off:off + 3 * D_DELTA])
    qkv_buf[HALO:HALO + ts, :] = qkv
    fullq = qkv_buf[...]
    w4 = dn_conv_w_ref[...]
    acc = (w4[0:1, :] * _shifted(fullq, 3) + w4[1:2, :] * _shifted(fullq, 2)
           + w4[2:3, :] * _shifted(fullq, 1) + w4[3:4, :] * qkv)
    qkv_buf[0:HALO, :] = qkv[ts - HALO:ts, :]
    act = _silu(acc)
    q = act[:, 0:D_DELTA]
    k = act[:, D_DELTA:2 * D_DELTA]
    q_s[...] = q * lax.rsqrt(_head_sums(q * q) + RMS_EPS) * (DN_HEAD_DIM ** -0.5)
    k_s[...] = k * lax.rsqrt(_head_sums(k * k) + RMS_EPS)
    v_s[...] = act[:, 2 * D_DELTA:3 * D_DELTA]

    ba = _dot(h, w_ba_ref[...])
    beta_all = _sigmoid(ba)
    a_in = ba + dt_row_ref[...]
    softplus = jnp.maximum(a_in, 0.0) + jnp.log1p(jnp.exp(-jnp.abs(a_in)))
    g_all = -jnp.exp(a_row_ref[...]) * softplus
    gc_all = _dot_split_rhs(tri_ref[...], g_all)
    beta_s[...] = _head_bcast(beta_all, 0)
    gc_s[...] = _head_bcast(gc_all, DN_HEADS)

    row = lax.broadcasted_iota(jnp.int32, (CHUNK, CHUNK), 0)
    col = lax.broadcasted_iota(jnp.int32, (CHUNK, CHUNK), 1)
    incl = row >= col
    eye = (row == col).astype(F32)
    blocks = [(slice(c * CHUNK, (c + 1) * CHUNK), slice(hd * DN_HEAD_DIM, (hd + 1) * DN_HEAD_DIM))
              for c in range(ts // CHUNK) for hd in range(DN_HEADS)]

    lmats, t_invs = [], []
    for rows, cols in blocks:
        qh = q_s[rows, cols]
        kh = k_s[rows, cols]
        gc = gc_s[rows, cols]
        gam = jnp.exp(jnp.where(incl, gc - gc.T, -jnp.inf))
        kb = kh.astype(BF16)
        lmat = beta_s[rows, cols] * _dot_nt(kb, kb) * gam
        qkg_s[rows, cols] = (_dot_nt(qh.astype(BF16), kb) * gam).astype(BF16)
        eg = jnp.exp(gc)
        qd_s[rows, cols] = (qh * eg).astype(BF16)
        k_end = kh * jnp.exp(gc[CHUNK - 1:CHUNK, :] - gc)
        kend_t_s[rows, cols] = k_end.T.astype(BF16)
        lmats.append(lmat)
        t_invs.append(eye - lmat * lvl_ref[0])
    for lv in range(1, N_LEVELS):
        tbs = [t.astype(BF16) for t in t_invs]
        ys = [_dot((lm * lvl_ref[lv]).astype(BF16), tb).astype(BF16) for lm, tb in zip(lmats, tbs)]
        t_invs = [t - _dot(tb, y) for t, tb, y in zip(t_invs, tbs, ys)]
    for (rows, cols), t_inv in zip(blocks, t_invs):
        beta = beta_s[rows, cols]
        kh = k_s[rows, cols]
        rhs = jnp.concatenate([beta * v_s[rows, cols], beta * jnp.exp(gc_s[rows, cols]) * kh], axis=1)
        sol = _dot(t_inv.astype(BF16), rhs.astype(BF16))
        u_s[rows, cols] = sol[:, 0:DN_HEAD_DIM]
        w_s[rows, cols] = sol[:, DN_HEAD_DIM:2 * DN_HEAD_DIM].astype(BF16)

    heads = [slice(hd * DN_HEAD_DIM, (hd + 1) * DN_HEAD_DIM) for hd in range(DN_HEADS)]
    for c in range(ts // CHUNK):
        rows = slice(c * CHUNK, (c + 1) * CHUNK)
        states = [s_ref[hd] for hd in range(DN_HEADS)]
        sbs = [st.astype(BF16) for st in states]
        dbs = [(u_s[rows, cols] - _dot(w_s[rows, cols], sb)).astype(BF16) for cols, sb in zip(heads, sbs)]
        for hd, cols in enumerate(heads):
            o_s[rows, cols] = _dot(qd_s[rows, cols], sbs[hd]) + _dot(qkg_s[rows, cols], dbs[hd])
            decay = jnp.exp(gc_s[(c + 1) * CHUNK - 1:(c + 1) * CHUNK, cols])
            s_ref[hd] = states[hd] * decay + _dot(kend_t_s[rows, cols], dbs[hd])

    o = o_s[...]
    z = _dot(h, w_main_ref[:, off + 3 * D_DELTA:off + 4 * D_DELTA])
    o_ms = _head_sums(o * o) * (1.0 / DN_HEAD_DIM)
    y_dn = o * lax.rsqrt(o_ms + RMS_EPS) * dn_nw_ref[...] * _silu(z)
    mix_in = jnp.concatenate([y_conv, y_dn], axis=1).astype(BF16)
    mix = _dot(mix_in, w_out_ref[...])
    r = DEEP_ALPHA * x + gate1 * mix
    mu = jnp.mean(r, axis=-1, keepdims=True)
    rc = r - mu
    var = jnp.mean(rc * rc, axis=-1, keepdims=True)
    x1 = rc * lax.rsqrt(var + LN_EPS) * ln_g_ref[...] + ln_b_ref[...]
    x1_ref[0] = x1
    h2 = x1 * (1.0 + scale2) + shift2
    _store_rows(h2_ref.at[0], h2)
    route, counts = _route_tile(_dot(h2.astype(BF16), w_rt_ref[...]) + b_rt_ref[...])
    route_ref[0] = route
    cnt_ref[0] = jnp.broadcast_to(counts, cnt_ref.shape[1:])


def _const_spec(shape):
    nd = len(shape)
    return pl.BlockSpec(shape, lambda b, j, _nd=nd: (0,) * _nd, pipeline_mode=pl.Buffered(1))


def _mixer_constants(ts):
    m = np.arange(MXU_TILE)
    g64 = (m[:, None] // CONV_GROUP == m[None, :] // CONV_GROUP).astype(np.float32)
    t = np.arange(ts)
    tri = ((t[:, None] // CHUNK == t[None, :] // CHUNK) & (t[None, :] <= t[:, None])).astype(np.float32)
    r = np.arange(CHUNK)
    lvl = np.stack([(((r[:, None] >> l) ^ (r[None, :] >> l)) == 1) & (r[:, None] > r[None, :])
                    for l in range(N_LEVELS)]).astype(np.float32)
    as_bf16 = lambda a: jnp.asarray(a, dtype=BF16)
    return as_bf16(g64), as_bf16(tri), jnp.asarray(lvl)


def _mixer(x, mod, w_in, conv_w, conv_norm_w, dn_conv_w, dn_a_log, dn_dt_bias, dn_norm_w, w_out,
           ln_g, ln_b, w_grp, b_grp, w_exp, b_exp):
    n_b, seq, d = x.shape
    ts = SEQ_TILE
    n_main = 3 * D_CONV + 4 * D_DELTA
    w_main = w_in[:, :n_main].astype(BF16)
    w_ba = jnp.pad(w_in[:, n_main:], ((0, 0), (0, LANES - 2 * DN_HEADS))).astype(BF16)
    head_pad = (DN_HEADS, LANES - 2 * DN_HEADS)
    a_row = jnp.pad(dn_a_log, head_pad).reshape(1, LANES)
    dt_row = jnp.pad(dn_dt_bias, head_pad).reshape(1, LANES)
    n_rt = N_GROUPS + N_EXPERTS
    w_rt = jnp.pad(jnp.concatenate([w_grp, w_exp], axis=1), ((0, 0), (0, LANES - n_rt))).astype(BF16)
    b_rt = jnp.pad(jnp.concatenate([b_grp, b_exp]), (0, LANES - n_rt)).reshape(1, LANES)
    consts = _mixer_constants(ts)
    operands = (
        x, mod, w_main, w_ba, conv_w, conv_norm_w.reshape(1, D_CONV), dn_conv_w, a_row, dt_row,
        jnp.tile(dn_norm_w, DN_HEADS).reshape(1, D_DELTA), w_out.astype(BF16),
        ln_g.reshape(1, d), ln_b.reshape(1, d), w_rt, b_rt) + consts
    in_specs = [
        pl.BlockSpec((1, ts, d), lambda b, j: (b, j, 0)),
        pl.BlockSpec((1,) + mod.shape[1:], lambda b, j: (b, 0, 0)),
    ] + [_const_spec(a.shape) for a in operands[2:]]
    tile_spec = lambda width: pl.BlockSpec((1, ts, width), lambda b, j: (b, j, 0))
    return pl.pallas_call(
        _mixer_kernel,
        grid=(n_b, seq // ts),
        in_specs=in_specs,
        out_specs=[tile_spec(d), pl.BlockSpec((1, ts * ROW_LINES, LANES), lambda b, j: (b, j, 0)),
                   tile_spec(LANES), pl.BlockSpec((1, SUBLANES, LANES), lambda b, j: (b, j, 0))],
        out_shape=[jax.ShapeDtypeStruct((n_b, seq, d), F32),
                   jax.ShapeDtypeStruct((n_b, seq * ROW_LINES, LANES), F32),
                   jax.ShapeDtypeStruct((n_b, seq, LANES), F32),
                   jax.ShapeDtypeStruct((n_b, seq // ts * SUBLANES, LANES), F32)],
        scratch_shapes=[
            pltpu.VMEM((HALO + ts, D_CONV), F32),
            pltpu.VMEM((HALO + ts, 3 * D_DELTA), F32),
            pltpu.VMEM((DN_HEADS, DN_HEAD_DIM, DN_HEAD_DIM), F32),
        ] + [pltpu.VMEM((ts, D_DELTA), F32)] * 7 + [pltpu.VMEM((ts, D_DELTA), BF16)] * 4,
        compiler_params=pltpu.CompilerParams(dimension_semantics=("arbitrary", "arbitrary"),
                                             vmem_limit_bytes=VMEM_LIMIT),
        name="mixer",
    )(*operands)


def _block_table(counts, n_asg):
    padded = ((counts + ROW_BLOCK - 1) // ROW_BLOCK) * ROW_BLOCK
    pend = jnp.cumsum(padded)
    pstart = pend - padded
    n_blocks = n_asg // ROW_BLOCK + N_EXPERTS
    block_start = jnp.arange(n_blocks, dtype=jnp.int32) * ROW_BLOCK
    block_expert = jnp.minimum(jnp.sum(pend[None, :] <= block_start[:, None], axis=1),
                               N_EXPERTS - 1).astype(jnp.int32)
    n_valid = jnp.clip((pstart + counts)[block_expert] - block_start, 0, ROW_BLOCK).astype(jnp.int32)
    n_used = (pend[-1:] // ROW_BLOCK).astype(jnp.int32)
    return pstart, block_expert, n_valid, n_used


def _pos_kernel(route_ref, pstart_ref, tri_ref, pos_ref, running):
    @pl.when(pl.program_id(0) == 0)
    def _():
        running[...] = jnp.zeros(running.shape, F32)

    rt = route_ref[...]
    lane = lax.broadcasted_iota(jnp.int32, rt.shape, 1).astype(F32)
    oh0 = (lane == rt[:, 0:1]).astype(F32)
    oh1 = (lane == rt[:, 1:2]).astype(F32)
    both = oh0 + oh1
    before = pstart_ref[...] + running[0:1, :] + _dot(tri_ref[...], both.astype(BF16))
    pos0 = jnp.sum(oh0 * before, axis=-1, keepdims=True)
    pos1 = jnp.sum(oh1 * before, axis=-1, keepdims=True)
    pos_ref[...] = jnp.where(lane == 0, pos0, jnp.where(lane == 1, pos1, 0.0)).astype(jnp.int32)
    running[...] = running[...] + jnp.sum(both, axis=0, keepdims=True)


def _positions(route, pstart):
    n_tok = route.shape[0]
    t = np.arange(POS_TILE)
    tri = jnp.asarray((t[None, :] < t[:, None]).astype(np.float32), dtype=BF16)
    pstart_row = jnp.pad(pstart.astype(F32), (0, LANES - N_EXPERTS)).reshape(1, LANES)
    spec = pl.BlockSpec((POS_TILE, LANES), lambda i: (i, 0))
    return pl.pallas_call(
        _pos_kernel,
        grid=(n_tok // POS_TILE,),
        in_specs=[spec, pl.BlockSpec((1, LANES), lambda i: (0, 0)),
                  pl.BlockSpec((POS_TILE, POS_TILE), lambda i: (0, 0), pipeline_mode=pl.Buffered(1))],
        out_specs=spec,
        out_shape=jax.ShapeDtypeStruct((n_tok, LANES), jnp.int32),
        scratch_shapes=[pltpu.VMEM((SUBLANES, LANES), F32)],
        compiler_params=pltpu.CompilerParams(dimension_semantics=("arbitrary",)),
        name="positions",
    )(route, pstart_row, tri)


def _dispatch_kernel(nv_ref, pos_ref, h2_ref, x_hbm, zbuf, sem_fill, sem_rows):
    i = pl.program_id(0)
    n_blocks = nv_ref.shape[0]
    n_tok_step = pos_ref.shape[2] // TOP_K

    def fill(b):
        return pltpu.make_async_copy(zbuf, x_hbm.at[_lines(b * ROW_BLOCK, ROW_BLOCK)], sem_fill)

    @pl.when(i == 0)
    def _():
        zbuf[...] = jnp.zeros(zbuf.shape, F32)

        def start_fill(b, carry):
            @pl.when(nv_ref[b] < ROW_BLOCK)
            def _():
                fill(b).start()
            return carry

        def wait_fill(b, carry):
            @pl.when(nv_ref[b] < ROW_BLOCK)
            def _():
                fill(b).wait()
            return carry

        lax.fori_loop(0, n_blocks, start_fill, 0)
        lax.fori_loop(0, n_blocks, wait_fill, 0)

    def group(g, carry):
        for j in range(DMA_UNROLL):
            t = g * DMA_UNROLL + j
            src = h2_ref.at[_lines(t)]
            for k in range(TOP_K):
                pltpu.make_async_copy(src, x_hbm.at[_lines(pos_ref[0, 0, t * TOP_K + k])],
                                      sem_rows).start(priority=k)
        return carry

    lax.fori_loop(0, n_tok_step // DMA_UNROLL, group, 0)
    for _ in range(TOP_K):
        pltpu.make_async_copy(h2_ref, x_hbm.at[pl.ds(0, h2_ref.shape[0])], sem_rows).wait()


def _dispatch(h2_lines, pos_steps, n_valid):
    n_blocks = n_valid.shape[0]
    step_lines = pos_steps.shape[2] // TOP_K * ROW_LINES
    grid_spec = pltpu.PrefetchScalarGridSpec(
        num_scalar_prefetch=1,
        grid=(pos_steps.shape[0],),
        in_specs=[pl.BlockSpec((1, 1, pos_steps.shape[2]), lambda i, nv: (i, 0, 0), memory_space=pltpu.SMEM),
                  pl.BlockSpec((step_lines, LANES), lambda i, nv: (i, 0))],
        out_specs=pl.BlockSpec(memory_space=pl.ANY),
        scratch_shapes=[pltpu.VMEM((ROW_BLOCK * ROW_LINES, LANES), F32),
                        pltpu.SemaphoreType.DMA(()), pltpu.SemaphoreType.DMA(())],
    )
    return pl.pallas_call(
        _dispatch_kernel,
        grid_spec=grid_spec,
        out_shape=jax.ShapeDtypeStruct((n_blocks * ROW_BLOCK * ROW_LINES, LANES), F32),
        compiler_params=pltpu.CompilerParams(dimension_semantics=("arbitrary",)),
        name="dispatch",
    )(n_valid, pos_steps, h2_lines)


def _expert_kernel(be_ref, nu_ref, x_ref, wg_ref, wu_ref, wd_ref, y_ref):
    del be_ref
    used = pl.program_id(0) < nu_ref[0]

    @pl.when(used)
    def _():
        xb = _load_rows(x_ref, ROW_BLOCK).astype(BF16)
        gate = _dot(xb, wg_ref[0])
        up = _dot(xb, wu_ref[0])
        _store_rows(y_ref, _dot((_silu(gate) * up).astype(BF16), wd_ref[0]))

    @pl.when(jnp.logical_not(used))
    def _():
        y_ref[...] = jnp.zeros(y_ref.shape, F32)


def _experts(x_lines, block_expert, n_used, w_gate, w_up, w_down):
    n_blocks = block_expert.shape[0]
    d = w_gate.shape[1]
    block_lines = ROW_BLOCK * ROW_LINES
    last_used = lambda i, nu: jnp.minimum(i, jnp.maximum(nu[0] - 1, 0))
    w_spec = lambda shape: pl.BlockSpec(shape, lambda i, be, nu: (be[last_used(i, nu)], 0, 0))
    grid_spec = pltpu.PrefetchScalarGridSpec(
        num_scalar_prefetch=2,
        grid=(n_blocks,),
        in_specs=[
            pl.BlockSpec((block_lines, LANES), lambda i, be, nu: (last_used(i, nu), 0)),
            w_spec((1, d, D_EXPERT)), w_spec((1, d, D_EXPERT)), w_spec((1, D_EXPERT, d)),
        ],
        out_specs=pl.BlockSpec((block_lines, LANES), lambda i, be, nu: (i, 0)),
    )
    return pl.pallas_call(
        _expert_kernel,
        grid_spec=grid_spec,
        out_shape=jax.ShapeDtypeStruct(x_lines.shape, F32),
        compiler_params=pltpu.CompilerParams(dimension_semantics=("arbitrary",)),
        name="experts",
    )(block_expert, n_used, x_lines, w_gate.astype(BF16), w_up.astype(BF16), w_down.astype(BF16))


def _final_kernel(pos_ref, pos_next_ref, x1_ref, mod_ref, route_ref, ln_g_ref, ln_b_ref, y_hbm, o_ref,
                  ybuf_a, ybuf_b, sem):
    step = pl.program_id(0)
    tt = FINAL_TILE
    gate2 = mod_ref[0, 5:6, :]
    ln_g = ln_g_ref[...]
    ln_b = ln_b_ref[...]

    def start_rows(idx_ref, half, lo, n, buf, sem_i):
        for t in range(lo, lo + n):
            for k in range(TOP_K):
                pltpu.make_async_copy(y_hbm.at[_lines(idx_ref[0, 0, (half * tt + t) * TOP_K + k])],
                                      buf.at[_lines(k * tt + t)], sem.at[sem_i]).start(priority=k)

    def wait_rows(buf, sem_i):
        pltpu.make_async_copy(y_hbm.at[pl.ds(0, buf.shape[0])], buf, sem.at[sem_i]).wait()

    def combine(half, lo, n, buf):
        rows = pl.ds(half * tt + lo, n)
        rt = route_ref[rows, :]
        y0 = _load_rows(buf.at[_lines(lo, n)], n)
        y1 = _load_rows(buf.at[_lines(tt + lo, n)], n)
        r = DEEP_ALPHA * x1_ref[0, rows, :] + gate2 * (rt[:, 2:3] * y0 + rt[:, 3:4] * y1)
        mu = jnp.mean(r, axis=-1, keepdims=True)
        rc = r - mu
        var = jnp.mean(rc * rc, axis=-1, keepdims=True)
        o_ref[0, rows, :] = rc * lax.rsqrt(var + LN_EPS) * ln_g + ln_b

    @pl.when(step == 0)
    def _():
        start_rows(pos_ref, 0, 0, tt, ybuf_a, 0)

    wait_rows(ybuf_a, 0)
    for lo in range(0, tt, FINAL_CHUNK):
        start_rows(pos_ref, 1, lo, FINAL_CHUNK, ybuf_b, 1)
        combine(0, lo, FINAL_CHUNK, ybuf_a)
    wait_rows(ybuf_b, 1)
    for lo in range(0, tt, FINAL_CHUNK):
        start_rows(pos_next_ref, 0, lo, FINAL_CHUNK, ybuf_a, 0)
        combine(1, lo, FINAL_CHUNK, ybuf_b)

    @pl.when(step == pl.num_programs(0) - 1)
    def _():
        wait_rows(ybuf_a, 0)


def _final(x1, mod, y_lines, route, pos, ln_g, ln_b):
    n_b, seq, d = x1.shape
    pair = 2 * FINAL_TILE
    per_b = seq // pair
    n_steps = n_b * per_b
    pos_pairs = pos.reshape(n_steps, 1, TOP_K * pair)
    pos_spec = lambda shift: pl.BlockSpec((1, 1, TOP_K * pair),
                                          lambda i: (jnp.minimum(i + shift, n_steps - 1), 0, 0),
                                          memory_space=pltpu.SMEM)
    buf = pltpu.VMEM((TOP_K * FINAL_TILE * ROW_LINES, LANES), F32)
    return pl.pallas_call(
        _final_kernel,
        grid=(n_steps,),
        in_specs=[
            pos_spec(0), pos_spec(1),
            pl.BlockSpec((1, pair, d), lambda i: (i // per_b, i % per_b, 0)),
            pl.BlockSpec((1,) + mod.shape[1:], lambda i: (i // per_b, 0, 0)),
            pl.BlockSpec((pair, LANES), lambda i: (i, 0)),
            pl.BlockSpec((1, d), lambda i: (0, 0)),
            pl.BlockSpec((1, d), lambda i: (0, 0)),
            pl.BlockSpec(memory_space=pl.ANY),
        ],
        out_specs=pl.BlockSpec((1, pair, d), lambda i: (i // per_b, i % per_b, 0)),
        out_shape=jax.ShapeDtypeStruct((n_b, seq, d), F32),
        scratch_shapes=[buf, buf, pltpu.SemaphoreType.DMA((2,))],
        compiler_params=pltpu.CompilerParams(dimension_semantics=("arbitrary",), vmem_limit_bytes=VMEM_LIMIT),
        name="final",
    )(pos_pairs, pos_pairs, x1, mod, route, ln_g.reshape(1, d), ln_b.reshape(1, d), y_lines)


def kernel(x, c, w_ada, b_ada, w_in, conv_w, conv_norm_w, dn_conv_w, dn_A_log, dn_dt_bias, dn_norm_w,
           w_out, ln1_g, ln1_b, w_grp, b_grp, w_exp, b_exp, w_gate, w_up, w_down, ln2_g, ln2_b):
    n_b, seq, d = x.shape
    n_tok = n_b * seq
    for l in range(w_ada.shape[0]):
        mod = _ada(c, w_ada[l], b_ada[l].reshape(1, -1)).reshape(n_b, 6, d)
        x1, h2, route, tile_counts = _mixer(x, mod, w_in[l], conv_w[l], conv_norm_w[l], dn_conv_w[l],
                                            dn_A_log[l], dn_dt_bias[l], dn_norm_w[l], w_out[l], ln1_g[l],
                                            ln1_b[l], w_grp[l], b_grp[l], w_exp[l], b_exp[l])
        route = route.reshape(n_tok, LANES)
        counts = jnp.sum(tile_counts.reshape(-1, SUBLANES, LANES)[:, 0, :N_EXPERTS], axis=0).astype(jnp.int32)
        pstart, block_expert, n_valid, n_used = _block_table(counts, n_tok * TOP_K)
        pos = _positions(route, pstart)[:, 0:TOP_K]
        pos_steps = pos.reshape(n_tok // DISPATCH_TILE, 1, TOP_K * DISPATCH_TILE)
        x_rows = _dispatch(h2.reshape(n_tok * ROW_LINES, LANES), pos_steps, n_valid)
        y_rows = _experts(x_rows, block_expert, n_used, w_gate[l], w_up[l], w_down[l])
        x = _final(x1, mod, y_rows, route, pos, ln2_g[l], ln2_b[l])
    return x
```
